```python
import math
import jax
import jax.numpy as jnp
from jax import lax
import numpy as np

D_MODEL = 2048
BATCH = 2
SEQ = 8192
DEPTH = 4

GRID_W = 64
CTX_LEN = 256
N_MOD = 6
EPS = 1e-6

HEAD_DIM = 128
RET_HEADS = 8
ATT_HEADS = 8
ATT_KV_HEADS = 2
RET_WIDTH = RET_HEADS * HEAD_DIM
ATT_WIDTH = ATT_HEADS * HEAD_DIM
ATT_KV_WIDTH = ATT_KV_HEADS * HEAD_DIM
EVEN_IN = 4 * RET_WIDTH + ATT_WIDTH + 2 * ATT_KV_WIDTH
EVEN_OUT = RET_WIDTH + ATT_WIDTH
RET_CHUNK = 128
Q_BLOCK = 128
ROPE_THETA = 10000.0
AXIS_DIM = HEAD_DIM // 2

SSM_EXPAND = 2
D_INNER = SSM_EXPAND * D_MODEL
SSM_HEAD_DIM = 64
SSM_HEADS = D_INNER // SSM_HEAD_DIM
D_STATE = 128
N_GROUPS = 8
HEADS_PER_GROUP = SSM_HEADS // N_GROUPS
CONV_DIM = D_INNER + 2 * N_GROUPS * D_STATE
D_CONV = 5
ODD_IN = D_INNER + CONV_DIM + 2 * SSM_HEADS
SSD_CHUNK = 128

PEER_HEADS = 8
N_KEYS = 128
N_EXPERTS = N_KEYS * N_KEYS
PEER_TOPK = 16
PEER_QDIM = 256
PEER_HALF = PEER_QDIM // 2
PEER_BLOCK = 128

kernel_name = 'hybrid_retention_gqa_ssd_peer_dit'


def rmsnorm(x, w):
    xf = x.astype(jnp.float32)
    y = xf * lax.rsqrt(jnp.mean(xf * xf, axis=-1, keepdims=True) + EPS)
    return (y * w.astype(jnp.float32)).astype(x.dtype)


def modulate(h, shift, scale):
    return h * (1 + scale[:, None, :]) + shift[:, None, :]


def flip(a):
    return jnp.flip(a, axis=1)


def to_chunks(a, size):
    b, t = a.shape[:2]
    return jnp.moveaxis(a.reshape((b, t // size, size) + a.shape[2:]), 1, 0)


def from_chunks(a):
    a = jnp.moveaxis(a, 0, 1)
    return a.reshape((a.shape[0], a.shape[1] * a.shape[2]) + a.shape[3:])


def rope_tables(n_tokens):
    rows = n_tokens // GRID_W
    row = jnp.repeat(jnp.arange(rows, dtype=jnp.float32), GRID_W)
    col = jnp.tile(jnp.arange(GRID_W, dtype=jnp.float32), rows)
    inv = ROPE_THETA ** (-jnp.arange(0, AXIS_DIM, 2, dtype=jnp.float32) / AXIS_DIM)
    ang_r = row[:, None] * inv[None, :]
    ang_c = col[:, None] * inv[None, :]
    return (jnp.cos(ang_r), jnp.sin(ang_r), jnp.cos(ang_c), jnp.sin(ang_c))


def _rotate(x, cos, sin):
    x1, x2 = jnp.split(x, 2, axis=-1)
    cos = cos[None, :, None, :]
    sin = sin[None, :, None, :]
    return jnp.concatenate([x1 * cos - x2 * sin, x1 * sin + x2 * cos], axis=-1)


def apply_rope2d(x, tabs):
    cos_r, sin_r, cos_c, sin_c = tabs
    xf = x.astype(jnp.float32)
    out = jnp.concatenate([_rotate(xf[..., :AXIS_DIM], cos_r, sin_r),
                           _rotate(xf[..., AXIS_DIM:], cos_c, sin_c)], axis=-1)
    return out.astype(x.dtype)


def retention_scan(q, k, v, log_gamma, state0):
    size = RET_CHUNK
    idx = jnp.arange(size, dtype=jnp.float32)
    diff = idx[:, None] - idx[None, :]
    lower = diff >= 0
    decay_mat = jnp.where(lower[None], jnp.exp(jnp.where(lower, diff, 0.0)[None] * log_gamma[:, None, None]), 0.0)
    q_decay = jnp.exp((idx + 1.0)[:, None] * log_gamma[None, :])
    k_decay = jnp.exp((size - 1.0 - idx)[:, None] * log_gamma[None, :])
    chunk_decay = jnp.exp(size * log_gamma)

    def step(state, inp):
        qc, kc, vc = inp
        scores = jnp.einsum('bihd,bjhd->bhij', qc, kc) * decay_mat[None]
        inner = jnp.einsum('bhij,bjhe->bihe', scores, vc)
        cross = jnp.einsum('bihd,bhde->bihe', qc, state) * q_decay[None, :, :, None]
        state = chunk_decay[None, :, None, None] * state + jnp.einsum(
            'bjhd,bjhe->bhde', kc * k_decay[None, :, :, None], vc)
        return state, inner + cross

    state, out = lax.scan(step, state0, (to_chunks(q, size), to_chunks(k, size), to_chunks(v, size)))
    return from_chunks(out), state


def bidir_retention(q, k, v, log_gamma, s_f, s_b):
    o_f, s_f = retention_scan(q, k, v, log_gamma[0], s_f)
    o_b, s_b = retention_scan(flip(q), flip(k), flip(v), log_gamma[1], s_b)
    return o_f + flip(o_b), s_f, s_b


def retention_readout(o, g):
    mu = jnp.mean(o, axis=-1, keepdims=True)
    var = jnp.mean(jnp.square(o - mu), axis=-1, keepdims=True)
    o = (o - mu) * lax.rsqrt(var + EPS)
    b, t = o.shape[:2]
    return (o.reshape(b, t, RET_WIDTH) * jax.nn.silu(g.astype(jnp.float32))).astype(g.dtype)


def block_attention(q, k, v):
    b, t = q.shape[:2]
    groups = ATT_HEADS // ATT_KV_HEADS
    scale = HEAD_DIM ** -0.5
    kf = k.astype(jnp.float32)
    vf = v.astype(jnp.float32)
    qb = to_chunks(q.reshape(b, t, ATT_KV_HEADS, groups, HEAD_DIM), Q_BLOCK)

    def one(qi):
        s = jnp.einsum('bqkgd,btkd->bkgqt', qi.astype(jnp.float32), kf) * scale
        p = jax.nn.softmax(s, axis=-1)
        return jnp.einsum('bkgqt,btkd->bqkgd', p, vf).astype(q.dtype)

    o = from_chunks(lax.map(one, qb))
    return o.reshape(b, t, ATT_WIDTH)


def retention_attention_mixer(h, hc, w_in, w_out, ret_decay, q_norm_w, k_norm_w, rope, need_ctx):
    splits = [RET_WIDTH, 2 * RET_WIDTH, 3 * RET_WIDTH, 4 * RET_WIDTH,
              4 * RET_WIDTH + ATT_WIDTH, 4 * RET_WIDTH + ATT_WIDTH + ATT_KV_WIDTH]

    def project(u):
        p = u @ w_in
        b, t = p.shape[:2]
        rq, rk, rv, rg, aq, ak, av = jnp.split(p, splits, axis=-1)
        heads = lambda a, n: a.reshape(b, t, n, HEAD_DIM)
        return (heads(rq, RET_HEADS), heads(rk, RET_HEADS) * (HEAD_DIM ** -0.5), heads(rv, RET_HEADS), rg,
                rmsnorm(heads(aq, ATT_HEADS), q_norm_w), rmsnorm(heads(ak, ATT_KV_HEADS), k_norm_w),
                heads(av, ATT_KV_HEADS))

    lq, lk, lv, lg, laq, lak, lav = project(h)
    cq, ck, cv, cg, caq, cak, cav = project(hc)
    lq, lk, laq, lak = [apply_rope2d(a, rope) for a in (lq, lk, laq, lak)]
    log_gamma = -jnp.exp(ret_decay.astype(jnp.float32))
    f = lambda a: a.astype(jnp.float32)
    b = h.shape[0]
    zero = jnp.zeros((b, RET_HEADS, HEAD_DIM, HEAD_DIM), jnp.float32)
    c_ret, s_f, s_b = bidir_retention(f(cq), f(ck), f(cv), log_gamma, zero, zero)
    l_ret, _, _ = bidir_retention(f(lq), f(lk), f(lv), log_gamma, s_f, s_b)
    l_att = block_attention(laq, jnp.concatenate([cak, lak], axis=1), jnp.concatenate([cav, lav], axis=1))
    y = jnp.concatenate([retention_readout(l_ret, lg), l_att], axis=-1) @ w_out
    if not need_ctx:
        return y, None
    c_att = block_attention(caq, cak, cav)
    yc = jnp.concatenate([retention_readout(c_ret, cg), c_att], axis=-1) @ w_out
    return y, yc


def depthwise_conv(u, w, bias):
    out = lax.conv_general_dilated(u, w[:, None, :], window_strides=(1,),
                                   padding=[(D_CONV // 2, D_CONV // 2)],
                                   dimension_numbers=('NWC', 'WIO', 'NWC'),
                                   feature_group_count=CONV_DIM)
    return out + bias


def ssd_scan(xs, dt, a, bm, cm, state0):
    size = SSD_CHUNK
    idx = jnp.arange(size)
    lower = idx[:, None] >= idx[None, :]
    a_g = a.reshape(N_GROUPS, HEADS_PER_GROUP)

    def step(state, inp):
        xc, dtc, bc, cc = inp
        bsz = xc.shape[0]
        xg = xc.reshape(bsz, size, N_GROUPS, HEADS_PER_GROUP, SSM_HEAD_DIM)
        dtg = dtc.reshape(bsz, size, N_GROUPS, HEADS_PER_GROUP)
        cs = jnp.cumsum(dtg * a_g, axis=1)
        seg = cs[:, :, None] - cs[:, None, :]
        m = lower[None, :, :, None, None]
        lmat = jnp.where(m, jnp.exp(jnp.where(m, seg, 0.0)), 0.0)
        cb = jnp.einsum('bign,bjgn->bgij', cc, bc)
        w = jnp.einsum('bgij,bijgh->bghij', cb, lmat) * jnp.moveaxis(dtg, 1, -1)[:, :, :, None, :]
        y_diag = jnp.einsum('bghij,bjghp->bighp', w, xg)
        y_off = jnp.einsum('bign,bghpn->bighp', cc, state) * jnp.exp(cs)[..., None]
        to_end = jnp.exp(cs[:, -1:] - cs) * dtg
        state = jnp.exp(cs[:, -1])[..., None, None] * state + jnp.einsum(
            'bjgn,bjgh,bjghp->bghpn', bc, to_end, xg)
        return state, (y_diag + y_off).reshape(bsz, size, SSM_HEADS, SSM_HEAD_DIM)

    state, y = lax.scan(step, state0, tuple(to_chunks(t, size) for t in (xs, dt, bm, cm)))
    return from_chunks(y), state


def bidir_ssd_mixer(h, hc, w_in, conv_w, conv_b, dt_bias, a_log, d_skip, norm_w, w_out, need_ctx):
    a = -jnp.exp(a_log.astype(jnp.float32))

    def project(u):
        p = u @ w_in
        b, t = p.shape[:2]
        z, xbc, dt = jnp.split(p, [D_INNER, D_INNER + CONV_DIM], axis=-1)
        xbc = jax.nn.silu(depthwise_conv(xbc, conv_w, conv_b))
        xs, bm, cm = jnp.split(xbc, [D_INNER, D_INNER + N_GROUPS * D_STATE], axis=-1)
        xs = xs.reshape(b, t, SSM_HEADS, SSM_HEAD_DIM).astype(jnp.float32)
        bm = bm.reshape(b, t, N_GROUPS, D_STATE).astype(jnp.float32)
        cm = cm.reshape(b, t, N_GROUPS, D_STATE).astype(jnp.float32)
        dt = jax.nn.softplus(dt.reshape(b, t, 2, SSM_HEADS).astype(jnp.float32) + dt_bias.astype(jnp.float32))
        return z, xs, bm, cm, dt

    def run(xs, bm, cm, dt, s_f, s_b):
        y_f, s_f = ssd_scan(xs, dt[:, :, 0], a[0], bm, cm, s_f)
        y_b, s_b = ssd_scan(flip(xs), flip(dt[:, :, 1]), a[1], flip(bm), flip(cm), s_b)
        y = y_f + flip(y_b) + d_skip.astype(jnp.float32)[:, None] * xs
        return y, s_f, s_b

    def readout(y, z):
        b, t = z.shape[:2]
        y = y.reshape(b, t, D_INNER) * jax.nn.silu(z.astype(jnp.float32))
        yg = y.reshape(b, t, N_GROUPS, D_INNER // N_GROUPS)
        yg = yg * lax.rsqrt(jnp.mean(yg * yg, axis=-1, keepdims=True) + EPS)
        return (yg.reshape(b, t, D_INNER) * norm_w.astype(jnp.float32)).astype(z.dtype) @ w_out

    cz, cx, cb, cc, cdt = project(hc)
    zero = jnp.zeros((h.shape[0], N_GROUPS, HEADS_PER_GROUP, SSM_HEAD_DIM, D_STATE), jnp.float32)
    yc, s_f, s_b = run(cx, cb, cc, cdt, zero, zero)
    lz, lx, lb, lc, ldt = project(h)
    yl, _, _ = run(lx, lb, lc, ldt, s_f, s_b)
    y = readout(yl, lz)
    if not need_ctx:
        return y, None
    return y, readout(yc, cz)


def peer_ffn(h, w_q, sub_keys, u, v):
    b, t, d = h.shape
    n = b * t
    hf = h.reshape(n, d)
    q = (hf @ w_q).reshape(n, PEER_HEADS, 2, PEER_HALF).astype(jnp.float32)
    s = jnp.einsum('nhsd,hskd->nhsk', q, sub_keys.astype(jnp.float32))
    s1, i1 = lax.top_k(s[:, :, 0], PEER_TOPK)
    s2, i2 = lax.top_k(s[:, :, 1], PEER_TOPK)
    cand = (s1[..., :, None] + s2[..., None, :]).reshape(n, PEER_HEADS, PEER_TOPK * PEER_TOPK)
    cand_idx = (i1[..., :, None] * N_KEYS + i2[..., None, :]).reshape(n, PEER_HEADS, PEER_TOPK * PEER_TOPK)
    best, pos = lax.top_k(cand, PEER_TOPK)
    expert = jnp.take_along_axis(cand_idx, pos, axis=-1)
    gate = jax.nn.softmax(best, axis=-1)
    nblk = n // PEER_BLOCK

    def one(args):
        hx, ex, gx = args
        act = jnp.einsum('nhed,nd->nhe', u[ex], hx)
        wgt = jax.nn.gelu(act.astype(jnp.float32), approximate=False) * gx
        return jnp.einsum('nhe,nhed->nd', wgt.astype(hx.dtype), v[ex])

    out = lax.map(one, (hf.reshape(nblk, PEER_BLOCK, d),
                        expert.reshape(nblk, PEER_BLOCK, PEER_HEADS, PEER_TOPK),
                        gate.reshape(nblk, PEER_BLOCK, PEER_HEADS, PEER_TOPK)))
    return out.reshape(b, t, d)


def setup_inputs(seed: int = 0) -> dict:
    key = jax.random.key(seed)
    keys = jax.random.split(key, 32)
    n_even = (DEPTH + 1) // 2
    n_odd = DEPTH // 2
    f32 = jnp.float32

    def nrm(i, shape, scale):
        return jax.random.normal(keys[i], shape, f32) * scale

    def gain(i, shape):
        return 1.0 + nrm(i, shape, 0.02)

    base_ret = jnp.log(-jnp.log1p(-(2.0 ** (-5.0 - jnp.arange(RET_HEADS, dtype=f32)))))
    dt0 = jnp.exp(jax.random.uniform(keys[17], (n_odd, 2, SSM_HEADS), f32, math.log(1e-3), math.log(1e-1)))
    return {
        'x': nrm(0, (BATCH, SEQ, D_MODEL), 1.0),
        'c': nrm(1, (BATCH, D_MODEL), 1.0),
        'ctx': nrm(2, (BATCH, CTX_LEN, D_MODEL), 1.0),
        'c_ctx': nrm(3, (D_MODEL,), 1.0),
        'ada_w': nrm(4, (DEPTH, D_MODEL, N_MOD * D_MODEL), 0.5 * D_MODEL ** -0.5),
        'ada_b': nrm(5, (DEPTH, N_MOD * D_MODEL), 0.02),
        'norm1_w': gain(6, (DEPTH, D_MODEL)),
        'norm2_w': gain(7, (DEPTH, D_MODEL)),
        'ev_w_in': nrm(8, (n_even, D_MODEL, EVEN_IN), D_MODEL ** -0.5),
        'ev_w_out': nrm(9, (n_even, EVEN_OUT, D_MODEL), EVEN_OUT ** -0.5),
        'ev_ret_decay': base_ret[None, None, :] + nrm(10, (n_even, 2, RET_HEADS), 0.05),
        'ev_q_norm': gain(11, (n_even, HEAD_DIM)),
        'ev_k_norm': gain(12, (n_even, HEAD_DIM)),
        'od_w_in': nrm(13, (n_odd, D_MODEL, ODD_IN), D_MODEL ** -0.5),
        'od_conv_w': nrm(14, (n_odd, D_CONV, CONV_DIM), D_CONV ** -0.5),
        'od_conv_b': nrm(15, (n_odd, CONV_DIM), 0.02),
        'od_dt_bias': dt0 + jnp.log(-jnp.expm1(-dt0)),
        'od_a_log': jnp.log(jax.random.uniform(keys[16], (n_odd, 2, SSM_HEADS), f32, 1.0, 16.0)),
        'od_d': gain(18, (n_odd, SSM_HEADS)),
        'od_norm_w': gain(19, (n_odd, D_INNER)),
        'od_w_out': nrm(20, (n_odd, D_INNER, D_MODEL), D_INNER ** -0.5),
        'peer_wq': nrm(21, (DEPTH, D_MODEL, PEER_HEADS * PEER_QDIM), D_MODEL ** -0.5),
        'peer_keys': nrm(22, (DEPTH, PEER_HEADS, 2, N_KEYS, PEER_HALF), PEER_HALF ** -0.5),
        'peer_u': nrm(23, (DEPTH, N_EXPERTS, D_MODEL), D_MODEL ** -0.5),
        'peer_v': nrm(24, (DEPTH, N_EXPERTS, D_MODEL), 0.5 / math.sqrt(PEER_HEADS)),
        'final_norm_w': gain(25, (D_MODEL,)),
    }


def reference(x, c, ctx, c_ctx, ada_w, ada_b, norm1_w, norm2_w, ev_w_in, ev_w_out, ev_ret_decay,
              ev_q_norm, ev_k_norm, od_w_in, od_conv_w, od_conv_b, od_dt_bias, od_a_log, od_d,
              od_norm_w, od_w_out, peer_wq, peer_keys, peer_u, peer_v, final_norm_w):
    rope = rope_tables(x.shape[1])
    xc = ctx
    sc = jax.nn.silu(c)
    scc = jax.nn.silu(c_ctx)[None]
    for layer in range(DEPTH):
        last = layer == DEPTH - 1
        mod = jnp.split(sc @ ada_w[layer] + ada_b[layer], N_MOD, axis=-1)
        modc = jnp.split(scc @ ada_w[layer] + ada_b[layer], N_MOD, axis=-1)
        h = modulate(rmsnorm(x, norm1_w[layer]), mod[0], mod[1])
        hc = modulate(rmsnorm(xc, norm1_w[layer]), modc[0], modc[1])
        j = layer // 2
        if layer % 2 == 0:
            y, yc = retention_attention_mixer(h, hc, ev_w_in[j], ev_w_out[j], ev_ret_decay[j],
                                              ev_q_norm[j], ev_k_norm[j], rope, not last)
        else:
            y, yc = bidir_ssd_mixer(h, hc, od_w_in[j], od_conv_w[j], od_conv_b[j], od_dt_bias[j],
                                    od_a_log[j], od_d[j], od_norm_w[j], od_w_out[j], not last)
        x = x + mod[2][:, None, :] * y
        h2 = modulate(rmsnorm(x, norm2_w[layer]), mod[3], mod[4])
        x = x + mod[5][:, None, :] * peer_ffn(h2, peer_wq[layer], peer_keys[layer], peer_u[layer], peer_v[layer])
        if not last:
            xc = xc + modc[2][:, None, :] * yc
            hc2 = modulate(rmsnorm(xc, norm2_w[layer]), modc[3], modc[4])
            xc = xc + modc[5][:, None, :] * peer_ffn(hc2, peer_wq[layer], peer_keys[layer], peer_u[layer], peer_v[layer])
    return rmsnorm(x, final_norm_w)
```

```python
import functools
import math

import jax
import jax.numpy as jnp
from jax import lax
from jax.experimental import pallas as pl
from jax.experimental.pallas import tpu as pltpu

F32 = jnp.float32
BF16 = jnp.bfloat16
I32 = jnp.int32

EPS = 1e-6
HEAD_DIM = 128
RET_HEADS = 8
ATT_HEADS = 8
ATT_KV_HEADS = 2
CHUNK = 128
GRID_W = 64
ROPE_THETA = 10000.0
SSM_HEAD_DIM = 64
D_STATE = 128
N_GROUPS = 8
D_CONV = 5
PEER_HEADS = 8
N_KEYS = 128
PEER_TOPK = 16
N_MOD = 6

LANES = 128
V7X_VMEM_BYTES = 64 * 1024 * 1024
VMEM_LIMIT = 52 * 1024 * 1024
NEG_BIG = -1e30

_NT = (((1,), (1,)), ((), ()))


def _pick(n, candidates):
    for c in candidates:
        if n % c == 0:
            return c
    raise ValueError(f"no tile for {n} in {candidates}")


def _cparams(sem, vmem=VMEM_LIMIT):
    return pltpu.CompilerParams(dimension_semantics=sem, vmem_limit_bytes=vmem)


def _sigmoid(x):
    return 1.0 / (1.0 + jnp.exp(-x))


def _silu(x):
    return x * _sigmoid(x)


def _softplus(x):
    return jnp.maximum(x, 0.0) + jnp.log(1.0 + jnp.exp(-jnp.abs(x)))


def _mods_kernel(c_ref, w_ref, b_ref, o_ref):
    sc = _silu(c_ref[...])
    acc = jnp.dot(sc.astype(BF16), w_ref[0].astype(BF16), preferred_element_type=F32)
    o_ref[0] = acc + b_ref[0]


def _mods(crow, ada_w, ada_b):
    depth, d, n = ada_w.shape
    tn = _pick(n, (1024, 512, 256, 128))
    return pl.pallas_call(
        _mods_kernel,
        out_shape=jax.ShapeDtypeStruct((depth, 8, n), F32),
        grid=(depth, n // tn),
        in_specs=[
            pl.BlockSpec((8, d), lambda l, j: (0, 0)),
            pl.BlockSpec((1, d, tn), lambda l, j: (l, 0, j)),
            pl.BlockSpec((1, 1, tn), lambda l, j: (l, 0, j)),
        ],
        out_specs=pl.BlockSpec((1, 8, tn), lambda l, j: (l, 0, j)),
        compiler_params=_cparams(("parallel", "parallel")),
        name="adaln_mods",
    )(crow, ada_w, ada_b.reshape(depth, 1, n))


def _norm_mod(x_ref, nw_ref, sh_ref, sc_ref, i, tm, ctx_len):
    x = x_ref[0]
    ms = jnp.mean(x * x, axis=-1, keepdims=True)
    y = x * lax.rsqrt(ms + EPS) * nw_ref[...]
    row = i * tm + lax.broadcasted_iota(I32, (tm, 1), 0)
    is_ctx = row < ctx_len
    sh = jnp.where(is_ctx, sh_ref[0, 0:1, :], sh_ref[0, 1:2, :])
    sc = jnp.where(is_ctx, sc_ref[0, 0:1, :], sc_ref[0, 1:2, :])
    return y * (1.0 + sc) + sh


def _rope(v, cos, sin_signed, lane_lo):
    rot = jnp.where(lane_lo, pltpu.roll(v, 96, 1), pltpu.roll(v, 32, 1))
    return v * cos + rot * sin_signed


def _head_rms(v, w):
    return v * lax.rsqrt(jnp.mean(v * v, axis=-1, keepdims=True) + EPS) * w


def _proj_plain_kernel(x_ref, nw_ref, sh_ref, sc_ref, w_ref, o_ref, *rest, tm, ctx_len, emit_h):
    if emit_h:
        h_out, h_scr = rest
    else:
        (h_scr,) = rest
    i = pl.program_id(1)
    j = pl.program_id(2)

    @pl.when(j == 0)
    def _():
        h = _norm_mod(x_ref, nw_ref, sh_ref, sc_ref, i, tm, ctx_len)
        h_scr[...] = h.astype(BF16)
        if emit_h:
            h_out[0] = h

    acc = jnp.dot(h_scr[...], w_ref[...], preferred_element_type=F32)
    o_ref[0] = acc.astype(o_ref.dtype)


def _proj_even_kernel(x_ref, nw_ref, sh_ref, sc_ref, w_ref, cos_ref, sin_ref, qn_ref, kn_ref,
                      o_ref, h_scr, *, tm, ctx_len, tn):
    i = pl.program_id(1)
    j = pl.program_id(2)

    @pl.when(j == 0)
    def _():
        h = _norm_mod(x_ref, nw_ref, sh_ref, sc_ref, i, tm, ctx_len)
        h_scr[...] = h.astype(BF16)

    acc = jnp.dot(h_scr[...], w_ref[...], preferred_element_type=F32)
    nh = tn // HEAD_DIM
    cos = cos_ref[...]
    sin = sin_ref[...]
    lane = lax.broadcasted_iota(I32, (1, HEAD_DIM), 1)
    lane_lo = (lane % 64) < 32
    k_scale = HEAD_DIM ** -0.5

    def heads(fn):
        for k in range(nh):
            sl = slice(k * HEAD_DIM, (k + 1) * HEAD_DIM)
            o_ref[0, :, sl] = fn(acc[:, sl], k).astype(o_ref.dtype)

    @pl.when(j < 2)
    def _():
        heads(lambda v, k: _rope(v, cos, sin, lane_lo))

    @pl.when((j >= 2) & (j < 4))
    def _():
        heads(lambda v, k: _rope(v, cos, sin, lane_lo) * k_scale)

    @pl.when((j >= 4) & (j < 8))
    def _():
        o_ref[0] = acc.astype(o_ref.dtype)

    @pl.when((j >= 8) & (j < 10))
    def _():
        heads(lambda v, k: _rope(_head_rms(v, qn_ref[...]), cos, sin, lane_lo))

    @pl.when(j == 10)
    def _():
        heads(lambda v, k: _rope(_head_rms(v, kn_ref[...]), cos, sin, lane_lo) if k < ATT_KV_HEADS else v)


def _proj(xa, nw, sh_sel, sc_sel, w, out_dtype, ctx_len, *, emit_h=False, even_extras=None):
    b, tt, d = xa.shape
    n = w.shape[1]
    tm = _pick(tt, (768, 384, 256, 128))
    tn = _pick(n, (512, 256, 128))
    grid = (b, tt // tm, n // tn)
    in_specs = [
        pl.BlockSpec((1, tm, d), lambda bb, i, j: (bb, i, 0)),
        pl.BlockSpec((1, d), lambda bb, i, j: (0, 0)),
        pl.BlockSpec((1, 2, d), lambda bb, i, j: (bb, 0, 0)),
        pl.BlockSpec((1, 2, d), lambda bb, i, j: (bb, 0, 0)),
        pl.BlockSpec((d, tn), lambda bb, i, j: (0, j)),
    ]
    args = [xa, nw, sh_sel, sc_sel, w]
    out_shape = jax.ShapeDtypeStruct((b, tt, n), out_dtype)
    out_specs = pl.BlockSpec((1, tm, tn), lambda bb, i, j: (bb, i, j))
    scratch = [pltpu.VMEM((tm, d), BF16)]
    if even_extras is not None:
        assert tn == 512 and n == 11 * 512
        cos, sin, qn, kn = even_extras
        in_specs += [
            pl.BlockSpec((tm, HEAD_DIM), lambda bb, i, j: (i, 0)),
            pl.BlockSpec((tm, HEAD_DIM), lambda bb, i, j: (i, 0)),
            pl.BlockSpec((1, HEAD_DIM), lambda bb, i, j: (0, 0)),
            pl.BlockSpec((1, HEAD_DIM), lambda bb, i, j: (0, 0)),
        ]
        args += [cos, sin, qn, kn]
        kern = functools.partial(_proj_even_kernel, tm=tm, ctx_len=ctx_len, tn=tn)
        name = "proj_even"
    else:
        kern = functools.partial(_proj_plain_kernel, tm=tm, ctx_len=ctx_len, emit_h=emit_h)
        name = "proj_plain"
        if emit_h:
            out_shape = (out_shape, jax.ShapeDtypeStruct((b, tt, d), F32))
            out_specs = (out_specs, pl.BlockSpec((1, tm, d), lambda bb, i, j: (bb, i, 0)))
    return pl.pallas_call(
        kern, out_shape=out_shape, grid=grid, in_specs=in_specs, out_specs=out_specs,
        scratch_shapes=scratch,
        compiler_params=_cparams(("parallel", "parallel", "arbitrary")),
        name=name,
    )(*args)


def _outproj_kernel(*refs, n_lhs, tm, ctx_len):
    lhs = refs[:n_lhs]
    ws = refs[n_lhs:2 * n_lhs]
    x_ref, g_ref, o_ref = refs[2 * n_lhs:]
    i = pl.program_id(1)
    acc = jnp.dot(lhs[0][0], ws[0][...], preferred_element_type=F32)
    for p in range(1, n_lhs):
        acc = acc + jnp.dot(lhs[p][0], ws[p][...], preferred_element_type=F32)
    row = i * tm + lax.broadcasted_iota(I32, (tm, 1), 0)
    g = jnp.where(row < ctx_len, g_ref[0, 0:1, :], g_ref[0, 1:2, :])
    o_ref[0] = x_ref[0] + g * acc


def _outproj(lhs_list, w_list, xa, g_sel, ctx_len):
    b, tt, d = xa.shape
    tm = _pick(tt, (768, 384, 256, 128))
    tn = _pick(d, (512, 256, 128))
    n_lhs = len(lhs_list)
    in_specs = []
    for l in lhs_list:
        in_specs.append(pl.BlockSpec((1, tm, l.shape[2]), lambda bb, i, j: (bb, i, 0)))
    for w in w_list:
        in_specs.append(pl.BlockSpec((w.shape[0], tn), lambda bb, i, j: (0, j)))
    in_specs += [
        pl.BlockSpec((1, tm, tn), lambda bb, i, j: (bb, i, j)),
        pl.BlockSpec((1, 2, tn), lambda bb, i, j: (bb, 0, j)),
    ]
    return pl.pallas_call(
        functools.partial(_outproj_kernel, n_lhs=n_lhs, tm=tm, ctx_len=ctx_len),
        out_shape=jax.ShapeDtypeStruct((b, tt, d), F32),
        grid=(b, tt // tm, d // tn),
        in_specs=in_specs,
        out_specs=pl.BlockSpec((1, tm, tn), lambda bb, i, j: (bb, i, j)),
        compiler_params=_cparams(("parallel", "parallel", "parallel")),
        name="outproj",
    )(*lhs_list, *w_list, xa, g_sel)


def _ret_kernel(*refs, direction, n_ctx_chunks):
    if direction == 0:
        q_ref, k_ref, v_ref, lg_ref, o_ref, st_scr, dm_scr, qd_scr, kd_scr = refs
    else:
        q_ref, k_ref, v_ref, lg_ref, of_ref, g_ref, o_ref, st_scr, dm_scr, qd_scr, kd_scr = refs
    c = pl.program_id(1)
    ii = lax.broadcasted_iota(I32, (CHUNK, CHUNK), 0).astype(F32)
    jj = lax.broadcasted_iota(I32, (CHUNK, CHUNK), 1).astype(F32)

    @pl.when(c == 0)
    def _():
        st_scr[...] = jnp.zeros_like(st_scr)
        for h in range(RET_HEADS):
            lg = -jnp.exp(lg_ref[h])
            if direction == 0:
                diff = ii - jj
                qe = ii + 1.0
                ke = (CHUNK - 1.0) - ii
            else:
                diff = jj - ii
                qe = CHUNK - ii
                ke = ii
            keep = diff >= 0
            dm_scr[h] = jnp.where(keep, jnp.exp(jnp.where(keep, diff, 0.0) * lg), 0.0)
            qd_scr[h] = jnp.exp(qe * lg)
            kd_scr[h] = jnp.exp(ke * lg)

    for h in range(RET_HEADS):
        sl = slice(h * HEAD_DIM, (h + 1) * HEAD_DIM)
        qh = q_ref[0, :, sl]
        kh = k_ref[0, :, sl]
        vh = v_ref[0, :, sl]
        st = st_scr[h]
        s = lax.dot_general(qh, kh, _NT, preferred_element_type=F32) * dm_scr[h]
        inner = jnp.dot(s.astype(BF16), vh, preferred_element_type=F32)
        cross = jnp.dot(qh, st.astype(BF16), preferred_element_type=F32) * qd_scr[h]
        out = inner + cross
        kdec = (kh.astype(F32) * kd_scr[h]).T.astype(BF16)
        cd = jnp.exp(CHUNK * (-jnp.exp(lg_ref[h])))
        st_scr[h] = cd * st + jnp.dot(kdec, vh, preferred_element_type=F32)
        if direction == 0:
            o_ref[0, :, sl] = out
        else:
            o = out + of_ref[0, :, sl]
            mu = jnp.mean(o, axis=-1, keepdims=True)
            var = jnp.mean(jnp.square(o - mu), axis=-1, keepdims=True)
            on = (o - mu) * lax.rsqrt(var + EPS)
            o_ref[0, :, sl] = (on * _silu(g_ref[0, :, sl].astype(F32))).astype(o_ref.dtype)


def _retention(p, lgb, ctx_len):
    b, tt, _ = p.shape
    nc = tt // CHUNK
    ncc = ctx_len // CHUNK
    w = RET_HEADS * HEAD_DIM

    def fwd_chunk(bb, c):
        return c

    def bwd_chunk(bb, c):
        return jnp.where(c < ncc, ncc - 1 - c, nc - 1 + ncc - c)

    def specs(chunk_fn):
        return [
            pl.BlockSpec((1, CHUNK, w), lambda bb, c: (bb, chunk_fn(bb, c), 0)),
            pl.BlockSpec((1, CHUNK, w), lambda bb, c: (bb, chunk_fn(bb, c), 1)),
            pl.BlockSpec((1, CHUNK, w), lambda bb, c: (bb, chunk_fn(bb, c), 2)),
        ]

    scratch = [pltpu.VMEM((RET_HEADS, HEAD_DIM, HEAD_DIM), F32) for _ in range(4)]
    lg_spec = lambda d: pl.BlockSpec((None, RET_HEADS, 1, LANES), lambda bb, c: (d, 0, 0, 0))
    o_f = pl.pallas_call(
        functools.partial(_ret_kernel, direction=0, n_ctx_chunks=ncc),
        out_shape=jax.ShapeDtypeStruct((b, tt, w), F32),
        grid=(b, nc),
        in_specs=specs(fwd_chunk) + [lg_spec(0)],
        out_specs=pl.BlockSpec((1, CHUNK, w), lambda bb, c: (bb, c, 0)),
        scratch_shapes=scratch,
        compiler_params=_cparams(("parallel", "arbitrary")),
        name="retention_fwd",
    )(p, p, p, lgb)
    y = pl.pallas_call(
        functools.partial(_ret_kernel, direction=1, n_ctx_chunks=ncc),
        out_shape=jax.ShapeDtypeStruct((b, tt, w), BF16),
        grid=(b, nc),
        in_specs=specs(bwd_chunk) + [
            lg_spec(1),
            pl.BlockSpec((1, CHUNK, w), lambda bb, c: (bb, bwd_chunk(bb, c), 0)),
            pl.BlockSpec((1, CHUNK, w), lambda bb, c: (bb, bwd_chunk(bb, c), 3)),
        ],
        out_specs=pl.BlockSpec((1, CHUNK, w), lambda bb, c: (bb, bwd_chunk(bb, c), 0)),
        scratch_shapes=scratch,
        compiler_params=_cparams(("parallel", "arbitrary")),
        name="retention_bwd",
    )(p, p, p, lgb, o_f, p)
    return y


def _attn_kernel(q_ref, k_ref, v_ref, o_ref, m_scr, l_scr, acc_scr, *, tq, tk, ctx_len, n_kc):
    i = pl.program_id(2)
    q = q_ref[0]
    scale = HEAD_DIM ** -0.5
    m_scr[...] = jnp.full_like(m_scr, NEG_BIG)
    l_scr[...] = jnp.zeros_like(l_scr)
    acc_scr[...] = jnp.zeros_like(acc_scr)

    def step(kc, masked):
        k = k_ref[0, pl.ds(pl.multiple_of(kc * tk, tk), tk), :]
        v = v_ref[0, pl.ds(pl.multiple_of(kc * tk, tk), tk), :]
        s = lax.dot_general(q, k, _NT, preferred_element_type=F32) * scale
        if masked:
            qrow = i * tq + lax.broadcasted_iota(I32, (tq, 1), 0)
            key = kc * tk + lax.broadcasted_iota(I32, (1, tk), 1)
            s = jnp.where((qrow < ctx_len) & (key >= ctx_len), NEG_BIG, s)
        m_old = m_scr[...]
        m_new = jnp.maximum(m_old, jnp.max(s, axis=-1, keepdims=True))
        alpha = jnp.exp(m_old - m_new)
        pexp = jnp.exp(s - m_new[:, 0:1])
        l_scr[...] = alpha * l_scr[...] + jnp.sum(pexp, axis=-1, keepdims=True)
        acc_scr[...] = alpha * acc_scr[...] + jnp.dot(pexp.astype(BF16), v, preferred_element_type=F32)
        m_scr[...] = m_new

    has_ctx = i * tq < ctx_len

    @pl.when(has_ctx)
    def _():
        def body(kc, carry):
            step(kc, True)
            return carry
        lax.fori_loop(0, n_kc, body, 0)

    @pl.when(jnp.logical_not(has_ctx))
    def _():
        def body(kc, carry):
            step(kc, False)
            return carry
        lax.fori_loop(0, n_kc, body, 0)

    o_ref[0] = (acc_scr[...] / l_scr[...]).astype(o_ref.dtype)


def _attention(p, ctx_len):
    b, tt, _ = p.shape
    tq = _pick(tt, (256, 128))
    tk = _pick(tt, (1408, 1152, 768, 384, 128))
    groups = ATT_HEADS // ATT_KV_HEADS
    q_base = 4 * RET_HEADS
    k_base = q_base + ATT_HEADS
    v_base = k_base + ATT_KV_HEADS
    return pl.pallas_call(
        functools.partial(_attn_kernel, tq=tq, tk=tk, ctx_len=ctx_len, n_kc=tt // tk),
        out_shape=jax.ShapeDtypeStruct((b, tt, ATT_HEADS * HEAD_DIM), BF16),
        grid=(b, ATT_KV_HEADS, tt // tq, groups),
        in_specs=[
            pl.BlockSpec((1, tq, HEAD_DIM), lambda bb, kv, i, g: (bb, i, q_base + kv * groups + g)),
            pl.BlockSpec((1, tt, HEAD_DIM), lambda bb, kv, i, g: (bb, 0, k_base + kv)),
            pl.BlockSpec((1, tt, HEAD_DIM), lambda bb, kv, i, g: (bb, 0, v_base + kv)),
        ],
        out_specs=pl.BlockSpec((1, tq, HEAD_DIM), lambda bb, kv, i, g: (bb, i, kv * groups + g)),
        scratch_shapes=[pltpu.VMEM((tq, LANES), F32), pltpu.VMEM((tq, LANES), F32),
                        pltpu.VMEM((tq, HEAD_DIM), F32)],
        compiler_params=_cparams(("parallel", "parallel", "parallel", "arbitrary")),
        name="gqa_attention",
    )(p, p, p)


def _conv_kernel(main_ref, prev_ref, next_ref, w_ref, b_ref, o_ref, *, tr, halo, ctx_len, tt):
    i = pl.program_id(1)
    start = i * tr
    seg_lo = jnp.where(start < ctx_len, 0, ctx_len)
    seg_hi = jnp.where(start < ctx_len, ctx_len, tt)
    ext = jnp.concatenate([prev_ref[0], main_ref[0], next_ref[0]], axis=0).astype(F32)
    row = start - halo + lax.broadcasted_iota(I32, (tr + 2 * halo, 1), 0)
    ext = jnp.where((row >= seg_lo) & (row < seg_hi), ext, 0.0)
    pad = D_CONV // 2
    acc = jnp.zeros((tr, ext.shape[1]), F32)
    for k in range(D_CONV):
        off = halo - pad + k
        acc = acc + w_ref[k:k + 1, :] * ext[off:off + tr, :]
    o_ref[0] = _silu(acc + b_ref[...]).astype(o_ref.dtype)


def _conv(p, col0, width, conv_w, conv_b, ctx_len):
    b, tt, _ = p.shape
    tr = CHUNK
    halo = 16
    tc = _pick(width, (512, 256, 128))
    assert ctx_len % tr == 0 and col0 % tc == 0
    cb0 = col0 // tc
    nblk = tt // halo
    r = tr // halo
    return pl.pallas_call(
        functools.partial(_conv_kernel, tr=tr, halo=halo, ctx_len=ctx_len, tt=tt),
        out_shape=jax.ShapeDtypeStruct((b, tt, width), BF16),
        grid=(b, tt // tr, width // tc),
        in_specs=[
            pl.BlockSpec((1, tr, tc), lambda bb, i, j: (bb, i, cb0 + j)),
            pl.BlockSpec((1, halo, tc), lambda bb, i, j: (bb, jnp.maximum(i * r - 1, 0), cb0 + j)),
            pl.BlockSpec((1, halo, tc), lambda bb, i, j: (bb, jnp.minimum((i + 1) * r, nblk - 1), cb0 + j)),
            pl.BlockSpec((D_CONV, tc), lambda bb, i, j: (0, j)),
            pl.BlockSpec((1, tc), lambda bb, i, j: (0, j)),
        ],
        out_specs=pl.BlockSpec((1, tr, tc), lambda bb, i, j: (bb, i, j)),
        compiler_params=_cparams(("parallel", "parallel", "parallel")),
        name="ssd_conv",
    )(p, p, p, conv_w, conv_b)


def _ssd_kernel(*refs, direction):
    if direction == 0:
        x_ref, b_ref, c_ref, dt_ref, bias_ref, alog_ref, o_ref, st_scr = refs
    else:
        (x_ref, b_ref, c_ref, dt_ref, bias_ref, alog_ref, yf_ref, z_ref, dsk_ref, nw_ref,
         o_ref, st_scr) = refs
    cstep = pl.program_id(1)

    @pl.when(cstep == 0)
    def _():
        st_scr[...] = jnp.zeros_like(st_scr)

    n_heads_dir = LANES // 2
    dt = _softplus(dt_ref[0] + bias_ref[...])
    dt_t = dt.T
    a_col = -jnp.exp(alog_ref[...])
    dta_t = dt_t * a_col
    kk = lax.broadcasted_iota(I32, (CHUNK, CHUNK), 0)
    jj = lax.broadcasted_iota(I32, (CHUNK, CHUNK), 1)
    if direction == 0:
        tri = (kk <= jj).astype(F32)
        keep = kk >= jj
        end_row = CHUNK - 1
    else:
        tri = (kk >= jj).astype(F32)
        keep = kk <= jj
        end_row = 0
    cs_t = jnp.dot(dta_t, tri, preferred_element_type=F32, precision=lax.Precision.HIGHEST)
    cs = cs_t.T
    lane = lax.broadcasted_iota(I32, (1, LANES), 1)
    lo = lane < SSM_HEAD_DIM
    hpg = n_heads_dir // N_GROUPS

    for g in range(N_GROUPS):
        gs = slice(g * D_STATE, (g + 1) * D_STATE)
        cmat = c_ref[0, :, gs]
        bmat = b_ref[0, :, gs]
        b_t = bmat.astype(F32).T.astype(BF16)
        cb = jnp.dot(cmat, b_t, preferred_element_type=F32)
        ysq = jnp.zeros((CHUNK, 1), F32)
        ypairs = []
        for pr in range(hpg // 2):
            h0 = g * hpg + 2 * pr
            hd0 = direction * n_heads_dir + h0
            xs = slice(h0 * SSM_HEAD_DIM, (h0 + 2) * SSM_HEAD_DIM)
            xpair = x_ref[0, :, xs]
            cs_b = []
            dt_b = []
            ydiag = []
            for u in range(2):
                hd = hd0 + u
                cs_col = jnp.broadcast_to(cs[:, hd:hd + 1], (CHUNK, LANES))
                dt_col = jnp.broadcast_to(dt[:, hd:hd + 1], (CHUNK, LANES))
                seg = cs_col - cs_t[hd:hd + 1, :]
                lm = jnp.where(keep, jnp.exp(jnp.where(keep, seg, 0.0)), 0.0)
                wmat = cb * lm * dt_t[hd:hd + 1, :]
                ydiag.append(jnp.dot(wmat.astype(BF16), xpair, preferred_element_type=F32))
                cs_b.append(cs_col)
                dt_b.append(dt_col)
            cs_pair = jnp.where(lo, cs_b[0], cs_b[1])
            dt_pair = jnp.where(lo, dt_b[0], dt_b[1])
            e_pair = jnp.exp(cs_pair)
            tot = cs_pair[end_row:end_row + 1, :]
            st = st_scr[g, pr]
            y_off = jnp.dot(cmat, st.astype(BF16), preferred_element_type=F32) * e_pair
            y = jnp.where(lo, ydiag[0], ydiag[1]) + y_off
            to_end = jnp.exp(tot - cs_pair) * dt_pair
            xw = (xpair.astype(F32) * to_end).astype(BF16)
            st_scr[g, pr] = jnp.exp(tot) * st + jnp.dot(b_t, xw, preferred_element_type=F32)
            if direction == 0:
                o_ref[0, :, xs] = y
            else:
                y = y + yf_ref[0, :, xs] + dsk_ref[:, xs] * xpair.astype(F32)
                y = y * _silu(z_ref[0, :, xs].astype(F32))
                ysq = ysq + jnp.sum(y * y, axis=-1, keepdims=True)
                ypairs.append((xs, y))
        if direction == 1:
            inv = lax.rsqrt(ysq / (hpg * SSM_HEAD_DIM) + EPS)
            for xs, y in ypairs:
                o_ref[0, :, xs] = (y * inv * nw_ref[:, xs]).astype(o_ref.dtype)


def _ssd(xbc, dt_raw, p, dt_bias_row, alog_b, dskip_row, norm_w_row, ctx_len):
    b, tt, _ = xbc.shape
    nc = tt // CHUNK
    ncc = ctx_len // CHUNK
    d_inner = dskip_row.shape[1]
    gw = N_GROUPS * D_STATE
    nb = d_inner // gw

    def fwd_chunk(c):
        return c

    def bwd_chunk(c):
        return jnp.where(c < ncc, ncc - 1 - c, nc - 1 + ncc - c)

    def specs(cf):
        return [
            pl.BlockSpec((1, CHUNK, d_inner), lambda bb, c: (bb, cf(c), 0)),
            pl.BlockSpec((1, CHUNK, gw), lambda bb, c: (bb, cf(c), nb)),
            pl.BlockSpec((1, CHUNK, gw), lambda bb, c: (bb, cf(c), nb + 1)),
            pl.BlockSpec((1, CHUNK, LANES), lambda bb, c: (bb, cf(c), 0)),
            pl.BlockSpec((1, LANES), lambda bb, c: (0, 0)),
            pl.BlockSpec((LANES, LANES), lambda bb, c: (0, 0)),
        ]

    hpg = (d_inner // SSM_HEAD_DIM) // N_GROUPS
    scratch = [pltpu.VMEM((N_GROUPS, hpg // 2, D_STATE, 2 * SSM_HEAD_DIM), F32)]
    y_f = pl.pallas_call(
        functools.partial(_ssd_kernel, direction=0),
        out_shape=jax.ShapeDtypeStruct((b, tt, d_inner), F32),
        grid=(b, nc),
        in_specs=specs(fwd_chunk),
        out_specs=pl.BlockSpec((1, CHUNK, d_inner), lambda bb, c: (bb, c, 0)),
        scratch_shapes=scratch,
        compiler_params=_cparams(("parallel", "arbitrary")),
        name="ssd_fwd",
    )(xbc, xbc, xbc, dt_raw, dt_bias_row, alog_b)
    yn = pl.pallas_call(
        functools.partial(_ssd_kernel, direction=1),
        out_shape=jax.ShapeDtypeStruct((b, tt, d_inner), BF16),
        grid=(b, nc),
        in_specs=specs(bwd_chunk) + [
            pl.BlockSpec((1, CHUNK, d_inner), lambda bb, c: (bb, bwd_chunk(c), 0)),
            pl.BlockSpec((1, CHUNK, d_inner), lambda bb, c: (bb, bwd_chunk(c), 0)),
            pl.BlockSpec((1, d_inner), lambda bb, c: (0, 0)),
            pl.BlockSpec((1, d_inner), lambda bb, c: (0, 0)),
        ],
        out_specs=pl.BlockSpec((1, CHUNK, d_inner), lambda bb, c: (bb, bwd_chunk(c), 0)),
        scratch_shapes=scratch,
        compiler_params=_cparams(("parallel", "arbitrary")),
        name="ssd_bwd",
    )(xbc, xbc, xbc, dt_raw, dt_bias_row, alog_b, y_f, p, dskip_row, norm_w_row)
    return yn


def _topk_rows(sc, n_rows, k):
    rid = lax.broadcasted_iota(I32, sc.shape, 0)
    vals = []
    idxs = []
    for _ in range(k):
        m = jnp.max(sc, axis=0, keepdims=True)
        idx = jnp.min(jnp.where(sc == m, rid, n_rows), axis=0, keepdims=True)
        vals.append(m)
        idxs.append(idx)
        sc = jnp.where(rid == idx, -jnp.inf, sc)
    return jnp.concatenate(vals, axis=0), jnp.concatenate(idxs, axis=0), rid


def _route_kernel(q_ref, keys_ref, ex_ref, gate_ref, ext_scr, gt_scr):
    def head_body(h, carry):
        side = []
        for s in range(2):
            col = pl.multiple_of((h * 2 + s) * HEAD_DIM, HEAD_DIM)
            qs = q_ref[:, pl.ds(col, HEAD_DIM)].astype(BF16)
            ky = keys_ref[h, s]
            sc = lax.dot_general(ky, qs, _NT, preferred_element_type=F32)
            v, ix, _ = _topk_rows(sc, N_KEYS, PEER_TOPK)
            side.append((v, ix))
        (s1, i1), (s2, i2) = side
        cand = jnp.concatenate([s1[a:a + 1, :] + s2 for a in range(PEER_TOPK)], axis=0)
        cidx = jnp.concatenate([i1[a:a + 1, :] * N_KEYS + i2 for a in range(PEER_TOPK)], axis=0)
        rid = lax.broadcasted_iota(I32, cand.shape, 0)
        best = []
        exps = []
        for _ in range(PEER_TOPK):
            m = jnp.max(cand, axis=0, keepdims=True)
            pos = jnp.min(jnp.where(cand == m, rid, PEER_TOPK * PEER_TOPK), axis=0, keepdims=True)
            hit = rid == pos
            exps.append(jnp.max(jnp.where(hit, cidx, -1), axis=0, keepdims=True))
            best.append(m)
            cand = jnp.where(hit, -jnp.inf, cand)
        best = jnp.concatenate(best, axis=0)
        e = jnp.exp(best - best[0:1, :])
        gate = e / jnp.sum(e, axis=0, keepdims=True)
        row0 = pl.multiple_of(h * PEER_TOPK, PEER_TOPK)
        ext_scr[pl.ds(row0, PEER_TOPK), :] = jnp.concatenate(exps, axis=0)
        gt_scr[pl.ds(row0, PEER_TOPK), :] = gate
        return carry

    lax.fori_loop(0, PEER_HEADS, head_body, 0)
    ex_ref[...] = ext_scr[...].T
    gate_ref[...] = gt_scr[...].T


def _route(q, keys_bf16):
    n, dq = q.shape
    tl = LANES
    npair = PEER_HEADS * PEER_TOPK
    return pl.pallas_call(
        _route_kernel,
        out_shape=(jax.ShapeDtypeStruct((n, npair), I32), jax.ShapeDtypeStruct((n, npair), F32)),
        grid=(n // tl,),
        in_specs=[
            pl.BlockSpec((tl, dq), lambda i: (i, 0)),
            pl.BlockSpec(keys_bf16.shape, lambda i: (0, 0, 0, 0)),
        ],
        out_specs=(pl.BlockSpec((tl, npair), lambda i: (i, 0)), pl.BlockSpec((tl, npair), lambda i: (i, 0))),
        scratch_shapes=[pltpu.VMEM((npair, tl), I32), pltpu.VMEM((npair, tl), F32)],
        compiler_params=_cparams(("parallel",)),
        name="peer_route",
    )(q, keys_bf16)


ROWS_PER_EXPERT = 16
N_SLOTS = 8


def _peer_kernel(ex_ref, h_ref, gate_ref, x_ref, gmod_ref, tab_ref, o_ref, buf, sems, *, tb, npair, d):
    rows = npair * ROWS_PER_EXPERT

    def slot_copy_all(slot):
        return pltpu.make_async_copy(tab_ref.at[pl.ds(0, rows), :], buf.at[slot], sems.at[slot])

    def issue(t, slot):
        for j in range(npair):
            e = ex_ref[t, j]
            src = tab_ref.at[pl.ds(pl.multiple_of(e * ROWS_PER_EXPERT, ROWS_PER_EXPERT), ROWS_PER_EXPERT), :]
            dst = buf.at[slot, pl.ds(j * ROWS_PER_EXPERT, ROWS_PER_EXPERT), :]
            pltpu.make_async_copy(src, dst, sems.at[slot]).start()

    for t0 in range(N_SLOTS):
        issue(t0, t0)

    def body(t, carry):
        slot = t % N_SLOTS
        slot_copy_all(slot).wait()
        hrow = h_ref[pl.ds(t, 1), :]
        act = jnp.zeros((8, 2 * npair), F32)
        for s in range(ROWS_PER_EXPERT):
            xs = pltpu.bitcast(buf[slot, pl.ds(s, npair, stride=ROWS_PER_EXPERT), :], BF16)
            hs = jnp.broadcast_to(hrow[:, s * LANES:(s + 1) * LANES], (8, LANES)).astype(BF16)
            act = act + lax.dot_general(hs, xs, _NT, preferred_element_type=F32)
        gl = 0.5 * act * (1.0 + lax.erf(act * (2.0 ** -0.5)))
        wgt = pltpu.roll(gl * gate_ref[pl.ds(t, 1), :], 1, 1).astype(BF16)
        outs = []
        for s in range(ROWS_PER_EXPERT):
            xs = pltpu.bitcast(buf[slot, pl.ds(s, npair, stride=ROWS_PER_EXPERT), :], BF16)
            outs.append(jnp.dot(wgt, xs, preferred_element_type=F32)[0:1, :])
        o_ref[pl.ds(t, 1), :] = x_ref[pl.ds(t, 1), :] + gmod_ref[0] * jnp.concatenate(outs, axis=1)

        @pl.when(t + N_SLOTS < tb)
        def _():
            issue(t + N_SLOTS, slot)

        return carry

    lax.fori_loop(0, tb, body, 0)


def _peer(ex, h2, gate_even, x2, g_sel, table, tt, ctx_len):
    n, d = h2.shape
    npair = ex.shape[1]
    tb = _pick(math.gcd(tt, ctx_len), (128, 64, 32, 16, 8))
    assert tb >= N_SLOTS
    g_rows = g_sel.reshape(-1, 1, d)

    def g_index(i):
        r = i * tb
        return ((r // tt) * 2 + jnp.where(r % tt < ctx_len, 0, 1), 0, 0)

    return pl.pallas_call(
        functools.partial(_peer_kernel, tb=tb, npair=npair, d=d),
        out_shape=jax.ShapeDtypeStruct((n, d), F32),
        grid=(n // tb,),
        in_specs=[
            pl.BlockSpec((tb, npair), lambda i: (i, 0), memory_space=pltpu.SMEM),
            pl.BlockSpec((tb, d), lambda i: (i, 0)),
            pl.BlockSpec((tb, 2 * npair), lambda i: (i, 0)),
            pl.BlockSpec((tb, d), lambda i: (i, 0)),
            pl.BlockSpec((1, 1, d), g_index),
            pl.BlockSpec(memory_space=pl.ANY),
        ],
        out_specs=pl.BlockSpec((tb, d), lambda i: (i, 0)),
        scratch_shapes=[pltpu.VMEM((N_SLOTS, npair * ROWS_PER_EXPERT, LANES), jnp.uint32),
                        pltpu.SemaphoreType.DMA((N_SLOTS,))],
        compiler_params=_cparams(("arbitrary",)),
        name="peer_experts",
    )(ex, h2, gate_even, x2, g_rows, table)


def _final_kernel(x_ref, w_ref, o_ref):
    x = x_ref[0]
    o_ref[0] = x * lax.rsqrt(jnp.mean(x * x, axis=-1, keepdims=True) + EPS) * w_ref[...]


def _final_norm(xa, w_row, ctx_len):
    b, tt, d = xa.shape
    t = tt - ctx_len
    tr = _pick(math.gcd(t, ctx_len), (256, 128))
    off = ctx_len // tr
    return pl.pallas_call(
        _final_kernel,
        out_shape=jax.ShapeDtypeStruct((b, t, d), F32),
        grid=(b, t // tr),
        in_specs=[pl.BlockSpec((1, tr, d), lambda bb, i: (bb, i + off, 0)),
                  pl.BlockSpec((1, d), lambda bb, i: (0, 0))],
        out_specs=pl.BlockSpec((1, tr, d), lambda bb, i: (bb, i, 0)),
        compiler_params=_cparams(("parallel", "parallel")),
        name="final_norm",
    )(xa, w_row)


def _rope_tables(t, ctx_len):
    tok = jnp.arange(t, dtype=I32)
    row = (tok // GRID_W).astype(F32)
    col = (tok % GRID_W).astype(F32)
    axis_dim = HEAD_DIM // 2
    inv = ROPE_THETA ** (-jnp.arange(0, axis_dim, 2, dtype=F32) / axis_dim)
    ang_r = row[:, None] * inv[None, :]
    ang_c = col[:, None] * inv[None, :]
    cos = jnp.concatenate([jnp.cos(ang_r), jnp.cos(ang_r), jnp.cos(ang_c), jnp.cos(ang_c)], axis=-1)
    sin = jnp.concatenate([-jnp.sin(ang_r), jnp.sin(ang_r), -jnp.sin(ang_c), jnp.sin(ang_c)], axis=-1)
    cos = jnp.concatenate([jnp.ones((ctx_len, HEAD_DIM), F32), cos], axis=0)
    sin = jnp.concatenate([jnp.zeros((ctx_len, HEAD_DIM), F32), sin], axis=0)
    return cos, sin


def _pack_experts(u, v):
    e, d = u.shape
    ub = lax.bitcast_convert_type(u.astype(BF16), jnp.uint16).astype(jnp.uint32)
    vb = lax.bitcast_convert_type(v.astype(BF16), jnp.uint16).astype(jnp.uint32)
    return (ub | (vb << 16)).reshape(e * (d // LANES), LANES)


def _mod_sel(mods_l, chunk, b, d):
    sl = slice(chunk * d, (chunk + 1) * d)
    ctx_row = jnp.broadcast_to(mods_l[b:b + 1, sl], (b, d))
    return jnp.stack([ctx_row, mods_l[:b, sl]], axis=1)


def kernel(x, c, ctx, c_ctx, ada_w, ada_b, norm1_w, norm2_w, ev_w_in, ev_w_out, ev_ret_decay, ev_q_norm, ev_k_norm, od_w_in, od_conv_w, od_conv_b, od_dt_bias, od_a_log, od_d, od_norm_w, od_w_out, peer_wq, peer_keys, peer_u, peer_v, final_norm_w):
    b, t, d = x.shape
    ctx_len = ctx.shape[1]
    depth = ada_w.shape[0]
    tt = ctx_len + t
    assert b + 1 <= 8 and ctx_len % CHUNK == 0 and t % CHUNK == 0 and t % GRID_W == 0

    xa = jnp.concatenate([ctx, x], axis=1)
    crow = jnp.zeros((8, d), F32).at[:b].set(c).at[b].set(c_ctx)
    mods = _mods(crow, ada_w, ada_b)
    cos, sin = _rope_tables(t, ctx_len)

    ret_w = RET_HEADS * HEAD_DIM
    d_inner = od_w_out.shape[1]
    n_ssm_heads = d_inner // SSM_HEAD_DIM
    zx_cols = 2 * d_inner + 2 * N_GROUPS * D_STATE

    for layer in range(depth):
        m = mods[layer]
        sel = [_mod_sel(m, k, b, d) for k in range(N_MOD)]
        j = layer // 2
        nw1 = norm1_w[layer].reshape(1, d)
        if layer % 2 == 0:
            p = _proj(xa, nw1, sel[0], sel[1], ev_w_in[j].astype(BF16), BF16, ctx_len,
                      even_extras=(cos, sin, ev_q_norm[j].reshape(1, HEAD_DIM), ev_k_norm[j].reshape(1, HEAD_DIM)))
            lgb = jnp.broadcast_to(ev_ret_decay[j].astype(F32)[:, :, None, None], (2, RET_HEADS, 1, LANES))
            y_ret = _retention(p, lgb, ctx_len)
            y_att = _attention(p, ctx_len)
            w_out = ev_w_out[j].astype(BF16)
            xa = _outproj([y_ret, y_att], [w_out[:ret_w], w_out[ret_w:]], xa, sel[2], ctx_len)
        else:
            w_in = od_w_in[j]
            p = _proj(xa, nw1, sel[0], sel[1], w_in[:, :zx_cols].astype(BF16), BF16, ctx_len)
            dt_raw = _proj(xa, nw1, sel[0], sel[1], w_in[:, zx_cols:].astype(BF16), F32, ctx_len)
            xbc = _conv(p, d_inner, zx_cols - d_inner, od_conv_w[j], od_conv_b[j].reshape(1, -1), ctx_len)
            alog_b = jnp.broadcast_to(od_a_log[j].astype(F32).reshape(2 * n_ssm_heads, 1), (2 * n_ssm_heads, LANES))
            dskip_row = jnp.repeat(od_d[j].astype(F32), SSM_HEAD_DIM).reshape(1, d_inner)
            yn = _ssd(xbc, dt_raw, p, od_dt_bias[j].reshape(1, 2 * n_ssm_heads), alog_b, dskip_row,
                      od_norm_w[j].reshape(1, d_inner), ctx_len)
            xa = _outproj([yn], [od_w_out[j].astype(BF16)], xa, sel[2], ctx_len)

        q, h2 = _proj(xa, norm2_w[layer].reshape(1, d), sel[3], sel[4], peer_wq[layer].astype(BF16), F32,
                      ctx_len, emit_h=True)
        n = b * tt
        ex, gate = _route(q.reshape(n, -1), peer_keys[layer].astype(BF16))
        gate_even = jnp.stack([gate, jnp.zeros_like(gate)], axis=-1).reshape(n, -1)
        table = _pack_experts(peer_u[layer], peer_v[layer])
        xa = _peer(ex, h2.reshape(n, d), gate_even, xa.reshape(n, d), sel[5], table, tt, ctx_len).reshape(b, tt, d)

    return _final_norm(xa, final_norm_w.reshape(1, d), ctx_len)
```

```python
import functools
import math

import jax
import jax.numpy as jnp
from jax import lax
from jax.experimental import pallas as pl
from jax.experimental.pallas import tpu as pltpu

F32 = jnp.float32
BF16 = jnp.bfloat16
I32 = jnp.int32

EPS = 1e-6
HEAD_DIM = 128
RET_HEADS = 8
ATT_HEADS = 8
ATT_KV_HEADS = 2
CHUNK = 128
GRID_W = 64
ROPE_THETA = 10000.0
SSM_HEAD_DIM = 64
D_STATE = 128
N_GROUPS = 8
D_CONV = 5
PEER_HEADS = 8
N_KEYS = 128
PEER_TOPK = 16
N_MOD = 6

LANES = 128
V7X_VMEM_BYTES = 64 * 1024 * 1024
VMEM_LIMIT = 52 * 1024 * 1024
NEG_BIG = -1e30

_NT = (((1,), (1,)), ((), ()))


def _pick(n, candidates):
    for c in candidates:
        if n % c == 0:
            return c
    raise ValueError(f"no tile for {n} in {candidates}")


def _cparams(sem, vmem=VMEM_LIMIT):
    return pltpu.CompilerParams(dimension_semantics=sem, vmem_limit_bytes=vmem)


def _sigmoid(x):
    return 1.0 / (1.0 + jnp.exp(-x))


def _silu(x):
    return x * _sigmoid(x)


def _softplus(x):
    return jnp.maximum(x, 0.0) + jnp.log(1.0 + jnp.exp(-jnp.abs(x)))


def _mods_kernel(c_ref, w_ref, b_ref, o_ref):
    sc = _silu(c_ref[...])
    acc = jnp.dot(sc.astype(BF16), w_ref[0].astype(BF16), preferred_element_type=F32)
    o_ref[0] = acc + b_ref[0]


def _mods(crow, ada_w, ada_b):
    depth, d, n = ada_w.shape
    tn = _pick(n, (1024, 512, 256, 128))
    return pl.pallas_call(
        _mods_kernel,
        out_shape=jax.ShapeDtypeStruct((depth, 8, n), F32),
        grid=(depth, n // tn),
        in_specs=[
            pl.BlockSpec((8, d), lambda l, j: (0, 0)),
            pl.BlockSpec((1, d, tn), lambda l, j: (l, 0, j)),
            pl.BlockSpec((1, 1, tn), lambda l, j: (l, 0, j)),
        ],
        out_specs=pl.BlockSpec((1, 8, tn), lambda l, j: (l, 0, j)),
        compiler_params=_cparams(("parallel", "parallel")),
        name="adaln_mods",
    )(crow, ada_w, ada_b.reshape(depth, 1, n))


def _norm_mod(x_ref, nw_ref, sh_ref, sc_ref, i, tm, ctx_len):
    x = x_ref[0]
    ms = jnp.mean(x * x, axis=-1, keepdims=True)
    y = x * lax.rsqrt(ms + EPS) * nw_ref[...]
    row = i * tm + lax.broadcasted_iota(I32, (tm, 1), 0)
    is_ctx = row < ctx_len
    sh = jnp.where(is_ctx, sh_ref[0, 0:1, :], sh_ref[0, 1:2, :])
    sc = jnp.where(is_ctx, sc_ref[0, 0:1, :], sc_ref[0, 1:2, :])
    return y * (1.0 + sc) + sh


def _rope(v, cos, sin_signed, lane_lo):
    rot = jnp.where(lane_lo, pltpu.roll(v, 96, 1), pltpu.roll(v, 32, 1))
    return v * cos + rot * sin_signed


def _head_rms(v, w):
    return v * lax.rsqrt(jnp.mean(v * v, axis=-1, keepdims=True) + EPS) * w


def _proj_plain_kernel(x_ref, nw_ref, sh_ref, sc_ref, w_ref, o_ref, *rest, tm, ctx_len, emit_h):
    if emit_h:
        h_out, h_scr = rest
    else:
        (h_scr,) = rest
    i = pl.program_id(1)
    j = pl.program_id(2)

    @pl.when(j == 0)
    def _():
        h = _norm_mod(x_ref, nw_ref, sh_ref, sc_ref, i, tm, ctx_len)
        h_scr[...] = h.astype(BF16)
        if emit_h:
            h_out[0] = h

    acc = jnp.dot(h_scr[...], w_ref[...], preferred_element_type=F32)
    o_ref[0] = acc.astype(o_ref.dtype)


def _proj_even_kernel(x_ref, nw_ref, sh_ref, sc_ref, w_ref, cos_ref, sin_ref, qn_ref, kn_ref,
                      o_ref, h_scr, *, tm, ctx_len, tn):
    i = pl.program_id(1)
    j = pl.program_id(2)

    @pl.when(j == 0)
    def _():
        h = _norm_mod(x_ref, nw_ref, sh_ref, sc_ref, i, tm, ctx_len)
        h_scr[...] = h.astype(BF16)

    acc = jnp.dot(h_scr[...], w_ref[...], preferred_element_type=F32)
    nh = tn // HEAD_DIM
    cos = cos_ref[...]
    sin = sin_ref[...]
    lane = lax.broadcasted_iota(I32, (1, HEAD_DIM), 1)
    lane_lo = (lane % 64) < 32
    k_scale = HEAD_DIM ** -0.5

    def heads(fn):
        for k in range(nh):
            sl = slice(k * HEAD_DIM, (k + 1) * HEAD_DIM)
            o_ref[0, :, sl] = fn(acc[:, sl], k).astype(o_ref.dtype)

    @pl.when(j < 2)
    def _():
        heads(lambda v, k: _rope(v, cos, sin, lane_lo))

    @pl.when((j >= 2) & (j < 4))
    def _():
        heads(lambda v, k: _rope(v, cos, sin, lane_lo) * k_scale)

    @pl.when((j >= 4) & (j < 8))
    def _():
        o_ref[0] = acc.astype(o_ref.dtype)

    @pl.when((j >= 8) & (j < 10))
    def _():
        heads(lambda v, k: _rope(_head_rms(v, qn_ref[...]), cos, sin, lane_lo))

    @pl.when(j == 10)
    def _():
        heads(lambda v, k: _rope(_head_rms(v, kn_ref[...]), cos, sin, lane_lo) if k < ATT_KV_HEADS else v)


def _proj(xa, nw, sh_sel, sc_sel, w, out_dtype, ctx_len, *, emit_h=False, even_extras=None):
    b, tt, d = xa.shape
    n = w.shape[1]
    tm = _pick(tt, (768, 384, 256, 128))
    tn = _pick(n, (512, 256, 128))
    grid = (b, tt // tm, n // tn)
    in_specs = [
        pl.BlockSpec((1, tm, d), lambda bb, i, j: (bb, i, 0)),
        pl.BlockSpec((1, d), lambda bb, i, j: (0, 0)),
        pl.BlockSpec((1, 2, d), lambda bb, i, j: (bb, 0, 0)),
        pl.BlockSpec((1, 2, d), lambda bb, i, j: (bb, 0, 0)),
        pl.BlockSpec((d, tn), lambda bb, i, j: (0, j)),
    ]
    args = [xa, nw, sh_sel, sc_sel, w]
    out_shape = jax.ShapeDtypeStruct((b, tt, n), out_dtype)
    out_specs = pl.BlockSpec((1, tm, tn), lambda bb, i, j: (bb, i, j))
    scratch = [pltpu.VMEM((tm, d), BF16)]
    if even_extras is not None:
        assert tn == 512 and n == 11 * 512
        cos, sin, qn, kn = even_extras
        in_specs += [
            pl.BlockSpec((tm, HEAD_DIM), lambda bb, i, j: (i, 0)),
            pl.BlockSpec((tm, HEAD_DIM), lambda bb, i, j: (i, 0)),
            pl.BlockSpec((1, HEAD_DIM), lambda bb, i, j: (0, 0)),
            pl.BlockSpec((1, HEAD_DIM), lambda bb, i, j: (0, 0)),
        ]
        args += [cos, sin, qn, kn]
        kern = functools.partial(_proj_even_kernel, tm=tm, ctx_len=ctx_len, tn=tn)
        name = "proj_even"
    else:
        kern = functools.partial(_proj_plain_kernel, tm=tm, ctx_len=ctx_len, emit_h=emit_h)
        name = "proj_plain"
        if emit_h:
            out_shape = (out_shape, jax.ShapeDtypeStruct((b, tt, d), F32))
            out_specs = (out_specs, pl.BlockSpec((1, tm, d), lambda bb, i, j: (bb, i, 0)))
    return pl.pallas_call(
        kern, out_shape=out_shape, grid=grid, in_specs=in_specs, out_specs=out_specs,
        scratch_shapes=scratch,
        compiler_params=_cparams(("parallel", "parallel", "arbitrary")),
        name=name,
    )(*args)


def _outproj_kernel(*refs, n_lhs, tm, ctx_len):
    lhs = refs[:n_lhs]
    ws = refs[n_lhs:2 * n_lhs]
    x_ref, g_ref, o_ref = refs[2 * n_lhs:]
    i = pl.program_id(1)
    acc = jnp.dot(lhs[0][0], ws[0][...], preferred_element_type=F32)
    for p in range(1, n_lhs):
        acc = acc + jnp.dot(lhs[p][0], ws[p][...], preferred_element_type=F32)
    row = i * tm + lax.broadcasted_iota(I32, (tm, 1), 0)
    g = jnp.where(row < ctx_len, g_ref[0, 0:1, :], g_ref[0, 1:2, :])
    o_ref[0] = x_ref[0] + g * acc


def _outproj(lhs_list, w_list, xa, g_sel, ctx_len):
    b, tt, d = xa.shape
    tm = _pick(tt, (768, 384, 256, 128))
    tn = _pick(d, (512, 256, 128))
    n_lhs = len(lhs_list)
    in_specs = []
    for l in lhs_list:
        in_specs.append(pl.BlockSpec((1, tm, l.shape[2]), lambda bb, i, j: (bb, i, 0)))
    for w in w_list:
        in_specs.append(pl.BlockSpec((w.shape[0], tn), lambda bb, i, j: (0, j)))
    in_specs += [
        pl.BlockSpec((1, tm, tn), lambda bb, i, j: (bb, i, j)),
        pl.BlockSpec((1, 2, tn), lambda bb, i, j: (bb, 0, j)),
    ]
    return pl.pallas_call(
        functools.partial(_outproj_kernel, n_lhs=n_lhs, tm=tm, ctx_len=ctx_len),
        out_shape=jax.ShapeDtypeStruct((b, tt, d), F32),
        grid=(b, tt // tm, d // tn),
        in_specs=in_specs,
        out_specs=pl.BlockSpec((1, tm, tn), lambda bb, i, j: (bb, i, j)),
        compiler_params=_cparams(("parallel", "parallel", "parallel")),
        name="outproj",
    )(*lhs_list, *w_list, xa, g_sel)


def _ret_kernel(*refs, direction, n_ctx_chunks):
    if direction == 0:
        q_ref, k_ref, v_ref, lg_ref, o_ref, st_scr, dm_scr, qd_scr, kd_scr = refs
    else:
        q_ref, k_ref, v_ref, lg_ref, of_ref, g_ref, o_ref, st_scr, dm_scr, qd_scr, kd_scr = refs
    c = pl.program_id(1)
    ii = lax.broadcasted_iota(I32, (CHUNK, CHUNK), 0).astype(F32)
    jj = lax.broadcasted_iota(I32, (CHUNK, CHUNK), 1).astype(F32)

    @pl.when(c == 0)
    def _():
        st_scr[...] = jnp.zeros_like(st_scr)
        for h in range(RET_HEADS):
            lg = -jnp.exp(lg_ref[h])
            if direction == 0:
                diff = ii - jj
                qe = ii + 1.0
                ke = (CHUNK - 1.0) - ii
            else:
                diff = jj - ii
                qe = CHUNK - ii
                ke = ii
            keep = diff >= 0
            dm_scr[h] = jnp.where(keep, jnp.exp(jnp.where(keep, diff, 0.0) * lg), 0.0)
            qd_scr[h] = jnp.exp(qe * lg)
            kd_scr[h] = jnp.exp(ke * lg)

    for h in range(RET_HEADS):
        sl = slice(h * HEAD_DIM, (h + 1) * HEAD_DIM)
        qh = q_ref[0, :, sl]
        kh = k_ref[0, :, sl]
        vh = v_ref[0, :, sl]
        st = st_scr[h]
        s = lax.dot_general(qh, kh, _NT, preferred_element_type=F32) * dm_scr[h]
        inner = jnp.dot(s.astype(BF16), vh, preferred_element_type=F32)
        cross = jnp.dot(qh, st.astype(BF16), preferred_element_type=F32) * qd_scr[h]
        out = inner + cross
        kdec = (kh.astype(F32) * kd_scr[h]).T.astype(BF16)
        cd = jnp.exp(CHUNK * (-jnp.exp(lg_ref[h])))
        st_scr[h] = cd * st + jnp.dot(kdec, vh, preferred_element_type=F32)
        if direction == 0:
            o_ref[0, :, sl] = out
        else:
            o = out + of_ref[0, :, sl]
            mu = jnp.mean(o, axis=-1, keepdims=True)
            var = jnp.mean(jnp.square(o - mu), axis=-1, keepdims=True)
            on = (o - mu) * lax.rsqrt(var + EPS)
            o_ref[0, :, sl] = (on * _silu(g_ref[0, :, sl].astype(F32))).astype(o_ref.dtype)


def _retention(p, lgb, ctx_len):
    b, tt, _ = p.shape
    nc = tt // CHUNK
    ncc = ctx_len // CHUNK
    w = RET_HEADS * HEAD_DIM

    def fwd_chunk(bb, c):
        return c

    def bwd_chunk(bb, c):
        return jnp.where(c < ncc, ncc - 1 - c, nc - 1 + ncc - c)

    def specs(chunk_fn):
        return [
            pl.BlockSpec((1, CHUNK, w), lambda bb, c: (bb, chunk_fn(bb, c), 0)),
            pl.BlockSpec((1, CHUNK, w), lambda bb, c: (bb, chunk_fn(bb, c), 1)),
            pl.BlockSpec((1, CHUNK, w), lambda bb, c: (bb, chunk_fn(bb, c), 2)),
        ]

    scratch = [pltpu.VMEM((RET_HEADS, HEAD_DIM, HEAD_DIM), F32) for _ in range(4)]
    lg_spec = lambda d: pl.BlockSpec((None, RET_HEADS, 1, LANES), lambda bb, c: (d, 0, 0, 0))
    o_f = pl.pallas_call(
        functools.partial(_ret_kernel, direction=0, n_ctx_chunks=ncc),
        out_shape=jax.ShapeDtypeStruct((b, tt, w), F32),
        grid=(b, nc),
        in_specs=specs(fwd_chunk) + [lg_spec(0)],
        out_specs=pl.BlockSpec((1, CHUNK, w), lambda bb, c: (bb, c, 0)),
        scratch_shapes=scratch,
        compiler_params=_cparams(("parallel", "arbitrary")),
        name="retention_fwd",
    )(p, p, p, lgb)
    y = pl.pallas_call(
        functools.partial(_ret_kernel, direction=1, n_ctx_chunks=ncc),
        out_shape=jax.ShapeDtypeStruct((b, tt, w), BF16),
        grid=(b, nc),
        in_specs=specs(bwd_chunk) + [
            lg_spec(1),
            pl.BlockSpec((1, CHUNK, w), lambda bb, c: (bb, bwd_chunk(bb, c), 0)),
            pl.BlockSpec((1, CHUNK, w), lambda bb, c: (bb, bwd_chunk(bb, c), 3)),
        ],
        out_specs=pl.BlockSpec((1, CHUNK, w), lambda bb, c: (bb, bwd_chunk(bb, c), 0)),
        scratch_shapes=scratch,
        compiler_params=_cparams(("parallel", "arbitrary")),
        name="retention_bwd",
    )(p, p, p, lgb, o_f, p)
    return y


def _attn_kernel(q_ref, k_ref, v_ref, o_ref, m_scr, l_scr, acc_scr, *, tq, tk, ctx_len, n_kc):
    i = pl.program_id(2)
    q = q_ref[0]
    scale = HEAD_DIM ** -0.5
    m_scr[...] = jnp.full_like(m_scr, NEG_BIG)
    l_scr[...] = jnp.zeros_like(l_scr)
    acc_scr[...] = jnp.zeros_like(acc_scr)

    def step(kc, masked):
        k = k_ref[0, pl.ds(pl.multiple_of(kc * tk, tk), tk), :]
        v = v_ref[0, pl.ds(pl.multiple_of(kc * tk, tk), tk), :]
        s = lax.dot_general(q, k, _NT, preferred_element_type=F32) * scale
        if masked:
            qrow = i * tq + lax.broadcasted_iota(I32, (tq, 1), 0)
            key = kc * tk + lax.broadcasted_iota(I32, (1, tk), 1)
            s = jnp.where((qrow < ctx_len) & (key >= ctx_len), NEG_BIG, s)
        m_old = m_scr[...]
        m_new = jnp.maximum(m_old, jnp.max(s, axis=-1, keepdims=True))
        alpha = jnp.exp(m_old - m_new)
        pexp = jnp.exp(s - m_new[:, 0:1])
        l_scr[...] = alpha * l_scr[...] + jnp.sum(pexp, axis=-1, keepdims=True)
        acc_scr[...] = alpha * acc_scr[...] + jnp.dot(pexp.astype(BF16), v, preferred_element_type=F32)
        m_scr[...] = m_new

    has_ctx = i * tq < ctx_len

    @pl.when(has_ctx)
    def _():
        def body(kc, carry):
            step(kc, True)
            return carry
        lax.fori_loop(0, n_kc, body, 0)

    @pl.when(jnp.logical_not(has_ctx))
    def _():
        def body(kc, carry):
            step(kc, False)
            return carry
        lax.fori_loop(0, n_kc, body, 0)

    o_ref[0] = (acc_scr[...] / l_scr[...]).astype(o_ref.dtype)


def _attention(p, ctx_len):
    b, tt, _ = p.shape
    tq = _pick(tt, (256, 128))
    tk = _pick(tt, (1408, 1152, 768, 384, 128))
    groups = ATT_HEADS // ATT_KV_HEADS
    q_base = 4 * RET_HEADS
    k_base = q_base + ATT_HEADS
    v_base = k_base + ATT_KV_HEADS
    return pl.pallas_call(
        functools.partial(_attn_kernel, tq=tq, tk=tk, ctx_len=ctx_len, n_kc=tt // tk),
        out_shape=jax.ShapeDtypeStruct((b, tt, ATT_HEADS * HEAD_DIM), BF16),
        grid=(b, ATT_KV_HEADS, tt // tq, groups),
        in_specs=[
            pl.BlockSpec((1, tq, HEAD_DIM), lambda bb, kv, i, g: (bb, i, q_base + kv * groups + g)),
            pl.BlockSpec((1, tt, HEAD_DIM), lambda bb, kv, i, g: (bb, 0, k_base + kv)),
            pl.BlockSpec((1, tt, HEAD_DIM), lambda bb, kv, i, g: (bb, 0, v_base + kv)),
        ],
        out_specs=pl.BlockSpec((1, tq, HEAD_DIM), lambda bb, kv, i, g: (bb, i, kv * groups + g)),
        scratch_shapes=[pltpu.VMEM((tq, LANES), F32), pltpu.VMEM((tq, LANES), F32),
                        pltpu.VMEM((tq, HEAD_DIM), F32)],
        compiler_params=_cparams(("parallel", "parallel", "parallel", "arbitrary")),
        name="gqa_attention",
    )(p, p, p)


def _conv_kernel(main_ref, prev_ref, next_ref, w_ref, b_ref, o_ref, *, tr, halo, ctx_len, tt):
    i = pl.program_id(1)
    start = i * tr
    seg_lo = jnp.where(start < ctx_len, 0, ctx_len)
    seg_hi = jnp.where(start < ctx_len, ctx_len, tt)
    ext = jnp.concatenate([prev_ref[0], main_ref[0], next_ref[0]], axis=0).astype(F32)
    row = start - halo + lax.broadcasted_iota(I32, (tr + 2 * halo, 1), 0)
    ext = jnp.where((row >= seg_lo) & (row < seg_hi), ext, 0.0)
    pad = D_CONV // 2
    acc = jnp.zeros((tr, ext.shape[1]), F32)
    for k in range(D_CONV):
        off = halo - pad + k
        acc = acc + w_ref[k:k + 1, :] * ext[off:off + tr, :]
    o_ref[0] = _silu(acc + b_ref[...]).astype(o_ref.dtype)


def _conv(p, col0, width, conv_w, conv_b, ctx_len):
    b, tt, _ = p.shape
    tr = CHUNK
    halo = 16
    tc = _pick(width, (512, 256, 128))
    assert ctx_len % tr == 0 and col0 % tc == 0
    cb0 = col0 // tc
    nblk = tt // halo
    r = tr // halo
    return pl.pallas_call(
        functools.partial(_conv_kernel, tr=tr, halo=halo, ctx_len=ctx_len, tt=tt),
        out_shape=jax.ShapeDtypeStruct((b, tt, width), BF16),
        grid=(b, tt // tr, width // tc),
        in_specs=[
            pl.BlockSpec((1, tr, tc), lambda bb, i, j: (bb, i, cb0 + j)),
            pl.BlockSpec((1, halo, tc), lambda bb, i, j: (bb, jnp.maximum(i * r - 1, 0), cb0 + j)),
            pl.BlockSpec((1, halo, tc), lambda bb, i, j: (bb, jnp.minimum((i + 1) * r, nblk - 1), cb0 + j)),
            pl.BlockSpec((D_CONV, tc), lambda bb, i, j: (0, j)),
            pl.BlockSpec((1, tc), lambda bb, i, j: (0, j)),
        ],
        out_specs=pl.BlockSpec((1, tr, tc), lambda bb, i, j: (bb, i, j)),
        compiler_params=_cparams(("parallel", "parallel", "parallel")),
        name="ssd_conv",
    )(p, p, p, conv_w, conv_b)


def _ssd_kernel(*refs, direction):
    if direction == 0:
        x_ref, b_ref, c_ref, dt_ref, bias_ref, alog_ref, o_ref, st_scr = refs
    else:
        (x_ref, b_ref, c_ref, dt_ref, bias_ref, alog_ref, yf_ref, z_ref, dsk_ref, nw_ref,
         o_ref, st_scr) = refs
    cstep = pl.program_id(1)

    @pl.when(cstep == 0)
    def _():
        st_scr[...] = jnp.zeros_like(st_scr)

    n_heads_dir = LANES // 2
    dt = _softplus(dt_ref[0] + bias_ref[...])
    dt_t = dt.T
    a_col = -jnp.exp(alog_ref[...])
    dta_t = dt_t * a_col
    kk = lax.broadcasted_iota(I32, (CHUNK, CHUNK), 0)
    jj = lax.broadcasted_iota(I32, (CHUNK, CHUNK), 1)
    if direction == 0:
        tri = (kk <= jj).astype(F32)
        keep = kk >= jj
        end_row = CHUNK - 1
    else:
        tri = (kk >= jj).astype(F32)
        keep = kk <= jj
        end_row = 0
    cs_t = jnp.dot(dta_t, tri, preferred_element_type=F32, precision=lax.Precision.HIGHEST)
    cs = cs_t.T
    lane = lax.broadcasted_iota(I32, (1, LANES), 1)
    lo = lane < SSM_HEAD_DIM
    hpg = n_heads_dir // N_GROUPS

    for g in range(N_GROUPS):
        gs = slice(g * D_STATE, (g + 1) * D_STATE)
        cmat = c_ref[0, :, gs]
        bmat = b_ref[0, :, gs]
        b_t = bmat.astype(F32).T.astype(BF16)
        cb = jnp.dot(cmat, b_t, preferred_element_type=F32)
        ysq = jnp.zeros((CHUNK, 1), F32)
        ypairs = []
        for pr in range(hpg // 2):
            h0 = g * hpg + 2 * pr
            hd0 = direction * n_heads_dir + h0
            xs = slice(h0 * SSM_HEAD_DIM, (h0 + 2) * SSM_HEAD_DIM)
            xpair = x_ref[0, :, xs]
            cs_b = []
            dt_b = []
            ydiag = []
            for u in range(2):
                hd = hd0 + u
                cs_col = jnp.broadcast_to(cs[:, hd:hd + 1], (CHUNK, LANES))
                dt_col = jnp.broadcast_to(dt[:, hd:hd + 1], (CHUNK, LANES))
                seg = cs_col - cs_t[hd:hd + 1, :]
                lm = jnp.where(keep, jnp.exp(jnp.where(keep, seg, 0.0)), 0.0)
                wmat = cb * lm * dt_t[hd:hd + 1, :]
                ydiag.append(jnp.dot(wmat.astype(BF16), xpair, preferred_element_type=F32))
                cs_b.append(cs_col)
                dt_b.append(dt_col)
            cs_pair = jnp.where(lo, cs_b[0], cs_b[1])
            dt_pair = jnp.where(lo, dt_b[0], dt_b[1])
            e_pair = jnp.exp(cs_pair)
            tot = cs_pair[end_row:end_row + 1, :]
            st = st_scr[g, pr]
            y_off = jnp.dot(cmat, st.astype(BF16), preferred_element_type=F32) * e_pair
            y = jnp.where(lo, ydiag[0], ydiag[1]) + y_off
            to_end = jnp.exp(tot - cs_pair) * dt_pair
            xw = (xpair.astype(F32) * to_end).astype(BF16)
            st_scr[g, pr] = jnp.exp(tot) * st + jnp.dot(b_t, xw, preferred_element_type=F32)
            if direction == 0:
                o_ref[0, :, xs] = y
            else:
                y = y + yf_ref[0, :, xs] + dsk_ref[:, xs] * xpair.astype(F32)
                y = y * _silu(z_ref[0, :, xs].astype(F32))
                ysq = ysq + jnp.sum(y * y, axis=-1, keepdims=True)
                ypairs.append((xs, y))
        if direction == 1:
            inv = lax.rsqrt(ysq / (hpg * SSM_HEAD_DIM) + EPS)
            for xs, y in ypairs:
                o_ref[0, :, xs] = (y * inv * nw_ref[:, xs]).astype(o_ref.dtype)


def _ssd(xbc, dt_raw, p, dt_bias_row, alog_b, dskip_row, norm_w_row, ctx_len):
    b, tt, _ = xbc.shape
    nc = tt // CHUNK
    ncc = ctx_len // CHUNK
    d_inner = dskip_row.shape[1]
    gw = N_GROUPS * D_STATE
    nb = d_inner // gw

    def fwd_chunk(c):
        return c

    def bwd_chunk(c):
        return jnp.where(c < ncc, ncc - 1 - c, nc - 1 + ncc - c)

    def specs(cf):
        return [
            pl.BlockSpec((1, CHUNK, d_inner), lambda bb, c: (bb, cf(c), 0)),
            pl.BlockSpec((1, CHUNK, gw), lambda bb, c: (bb, cf(c), nb)),
            pl.BlockSpec((1, CHUNK, gw), lambda bb, c: (bb, cf(c), nb + 1)),
            pl.BlockSpec((1, CHUNK, LANES), lambda bb, c: (bb, cf(c), 0)),
            pl.BlockSpec((1, LANES), lambda bb, c: (0, 0)),
            pl.BlockSpec((LANES, LANES), lambda bb, c: (0, 0)),
        ]

    hpg = (d_inner // SSM_HEAD_DIM) // N_GROUPS
    scratch = [pltpu.VMEM((N_GROUPS, hpg // 2, D_STATE, 2 * SSM_HEAD_DIM), F32)]
    y_f = pl.pallas_call(
        functools.partial(_ssd_kernel, direction=0),
        out_shape=jax.ShapeDtypeStruct((b, tt, d_inner), F32),
        grid=(b, nc),
        in_specs=specs(fwd_chunk),
        out_specs=pl.BlockSpec((1, CHUNK, d_inner), lambda bb, c: (bb, c, 0)),
        scratch_shapes=scratch,
        compiler_params=_cparams(("parallel", "arbitrary")),
        name="ssd_fwd",
    )(xbc, xbc, xbc, dt_raw, dt_bias_row, alog_b)
    yn = pl.pallas_call(
        functools.partial(_ssd_kernel, direction=1),
        out_shape=jax.ShapeDtypeStruct((b, tt, d_inner), BF16),
        grid=(b, nc),
        in_specs=specs(bwd_chunk) + [
            pl.BlockSpec((1, CHUNK, d_inner), lambda bb, c: (bb, bwd_chunk(c), 0)),
            pl.BlockSpec((1, CHUNK, d_inner), lambda bb, c: (bb, bwd_chunk(c), 0)),
            pl.BlockSpec((1, d_inner), lambda bb, c: (0, 0)),
            pl.BlockSpec((1, d_inner), lambda bb, c: (0, 0)),
        ],
        out_specs=pl.BlockSpec((1, CHUNK, d_inner), lambda bb, c: (bb, bwd_chunk(c), 0)),
        scratch_shapes=scratch,
        compiler_params=_cparams(("parallel", "arbitrary")),
        name="ssd_bwd",
    )(xbc, xbc, xbc, dt_raw, dt_bias_row, alog_b, y_f, p, dskip_row, norm_w_row)
    return yn


def _topk_rows(sc, n_rows, k):
    rid = lax.broadcasted_iota(I32, sc.shape, 0)
    vals = []
    idxs = []
    for _ in range(k):
        m = jnp.max(sc, axis=0, keepdims=True)
        idx = jnp.min(jnp.where(sc == m, rid, n_rows), axis=0, keepdims=True)
        vals.append(m)
        idxs.append(idx)
        sc = jnp.where(rid == idx, -jnp.inf, sc)
    return jnp.concatenate(vals, axis=0), jnp.concatenate(idxs, axis=0), rid


def _route_kernel(q_ref, keys_ref, ex_ref, gate_ref, ext_scr, gt_scr):
    def head_body(h, carry):
        side = []
        for s in range(2):
            col = pl.multiple_of((h * 2 + s) * HEAD_DIM, HEAD_DIM)
            qs = q_ref[:, pl.ds(col, HEAD_DIM)].astype(BF16)
            ky = keys_ref[h, s]
            sc = lax.dot_general(ky, qs, _NT, preferred_element_type=F32)
            v, ix, _ = _topk_rows(sc, N_KEYS, PEER_TOPK)
            side.append((v, ix))
        (s1, i1), (s2, i2) = side
        r8 = lax.broadcasted_iota(I32, (8, 1), 0)
        blocks = []
        for half in range(2):
            rows = slice(8 * half, 8 * half + 8)
            blocks.append((s1[0:1, :] + s2[rows, :], i1[0:1, :] * N_KEYS + i2[rows, :], 8 * half + r8))
        for a in range(1, 8):
            nb = PEER_TOPK // (a + 1)
            val = s1[a:a + 1, :] + s2[0:8, :]
            if nb < 8:
                val = jnp.where(r8 < nb, val, -jnp.inf)
            blocks.append((val, i1[a:a + 1, :] * N_KEYS + i2[0:8, :], PEER_TOPK * a + r8))
        blocks.append((s1[8:16, :] + s2[0:1, :], i1[8:16, :] * N_KEYS + i2[0:1, :],
                       PEER_TOPK * (8 + r8)))
        cand = jnp.concatenate([blk[0] for blk in blocks], axis=0)
        cidx = jnp.concatenate([blk[1] for blk in blocks], axis=0)
        flat = jnp.concatenate([blk[2] for blk in blocks], axis=0)
        best = []
        exps = []
        for _ in range(PEER_TOPK):
            m = jnp.max(cand, axis=0, keepdims=True)
            pos = jnp.min(jnp.where(cand == m, flat, PEER_TOPK * PEER_TOPK), axis=0, keepdims=True)
            hit = flat == pos
            exps.append(jnp.max(jnp.where(hit, cidx, -1), axis=0, keepdims=True))
            best.append(m)
            cand = jnp.where(hit, -jnp.inf, cand)
        best = jnp.concatenate(best, axis=0)
        e = jnp.exp(best - best[0:1, :])
        gate = e / jnp.sum(e, axis=0, keepdims=True)
        row0 = pl.multiple_of(h * PEER_TOPK, PEER_TOPK)
        ext_scr[pl.ds(row0, PEER_TOPK), :] = jnp.concatenate(exps, axis=0)
        gt_scr[pl.ds(row0, PEER_TOPK), :] = gate
        return carry

    lax.fori_loop(0, PEER_HEADS, head_body, 0)
    ex_ref[...] = ext_scr[...].T
    gate_ref[...] = gt_scr[...].T


def _route(q, keys_bf16):
    n, dq = q.shape
    tl = LANES
    npair = PEER_HEADS * PEER_TOPK
    return pl.pallas_call(
        _route_kernel,
        out_shape=(jax.ShapeDtypeStruct((n, npair), I32), jax.ShapeDtypeStruct((n, npair), F32)),
        grid=(n // tl,),
        in_specs=[
            pl.BlockSpec((tl, dq), lambda i: (i, 0)),
            pl.BlockSpec(keys_bf16.shape, lambda i: (0, 0, 0, 0)),
        ],
        out_specs=(pl.BlockSpec((tl, npair), lambda i: (i, 0)), pl.BlockSpec((tl, npair), lambda i: (i, 0))),
        scratch_shapes=[pltpu.VMEM((npair, tl), I32), pltpu.VMEM((npair, tl), F32)],
        compiler_params=_cparams(("parallel",)),
        name="peer_route",
    )(q, keys_bf16)


ROWS_PER_EXPERT = 16
N_SLOTS = 8
GROUP = 8


def _peer_kernel(ex_ref, h_ref, gate_ref, x_ref, gmod_ref, tab_ref, o_ref, buf, sems, w_scr, *, tb, npair):
    rpe = ROWS_PER_EXPERT
    rows = npair * rpe
    n_groups = npair // GROUP
    sub = lax.broadcasted_iota(I32, (8, LANES), 0)
    lane = lax.broadcasted_iota(I32, (8, LANES), 1)
    masks = [(sub & sh) == 0 for sh in (1, 2, 4)]
    hi_mask = jnp.uint32(0xFFFF0000)

    def wait_slot(slot):
        pltpu.make_async_copy(tab_ref.at[pl.ds(0, rows), :], buf.at[slot], sems.at[slot]).wait()

    def issue(t, slot, j0, j1):
        for j in range(j0, j1):
            e = ex_ref[t, j]
            src = tab_ref.at[pl.ds(pl.multiple_of(e * rpe, rpe), rpe), :]
            pltpu.make_async_copy(src, buf.at[slot, pl.ds(j * rpe, rpe), :], sems.at[slot]).start(priority=j % 2)

    def combine(a, b, level):
        m = masks[level]
        return jnp.where(m, a, b) + pltpu.roll(jnp.where(m, b, a), 1 << level, 0)

    def token(t, refill):
        slot = t % N_SLOTS
        wait_slot(slot)
        r0 = pl.multiple_of(t * rpe, rpe)
        h0 = h_ref[pl.ds(r0, 8), :]
        h1 = h_ref[pl.ds(r0 + 8, 8), :]
        grow = jnp.broadcast_to(gate_ref[pl.ds(t, 1), :], (8, LANES))
        acc0 = jnp.zeros((8, LANES), F32)
        acc1 = jnp.zeros((8, LANES), F32)
        for g in range(n_groups):
            if refill:
                issue(t + N_SLOTS - 1, (t + N_SLOTS - 1) % N_SLOTS, g * GROUP, (g + 1) * GROUP)
            parts = []
            for k in range(GROUP):
                base = (g * GROUP + k) * rpe
                u0 = pltpu.bitcast(buf[slot, pl.ds(base, 8), :] << 16, F32)
                u1 = pltpu.bitcast(buf[slot, pl.ds(base + 8, 8), :] << 16, F32)
                parts.append(u0 * h0 + u1 * h1)
            for level in range(3):
                parts = [combine(parts[2 * i], parts[2 * i + 1], level) for i in range(len(parts) // 2)]
            act = jnp.sum(parts[0], axis=-1, keepdims=True)
            gcol = jnp.sum(jnp.where(lane == g * GROUP + sub, grow, 0.0), axis=-1, keepdims=True)
            wcol = 0.5 * act * (1.0 + lax.erf(act * (2.0 ** -0.5))) * gcol
            w_scr[pl.ds(g * GROUP, GROUP), :] = jnp.broadcast_to(wcol, (GROUP, LANES))
            for k in range(GROUP):
                base = (g * GROUP + k) * rpe
                wb = jnp.broadcast_to(w_scr[pl.ds(g * GROUP + k, 1), :], (8, LANES))
                v0 = pltpu.bitcast(buf[slot, pl.ds(base, 8), :] & hi_mask, F32)
                v1 = pltpu.bitcast(buf[slot, pl.ds(base + 8, 8), :] & hi_mask, F32)
                acc0 = acc0 + wb * v0
                acc1 = acc1 + wb * v1
        o_ref[pl.ds(r0, 8), :] = x_ref[pl.ds(r0, 8), :] + gmod_ref[0:8, :] * acc0
        o_ref[pl.ds(r0 + 8, 8), :] = x_ref[pl.ds(r0 + 8, 8), :] + gmod_ref[8:16, :] * acc1

    for t0 in range(N_SLOTS - 1):
        issue(t0, t0, 0, npair)

    def main_body(t, carry):
        token(t, True)
        return carry

    def tail_body(t, carry):
        token(t, False)
        return carry

    lax.fori_loop(0, tb - (N_SLOTS - 1), main_body, 0)
    lax.fori_loop(tb - (N_SLOTS - 1), tb, tail_body, 0)


def _peer(ex, h2, gate, x2, g_sel, table, tt, ctx_len):
    n, npair = ex.shape
    rpe = ROWS_PER_EXPERT
    tb = _pick(math.gcd(tt, ctx_len), (128, 64, 32, 16, 8))
    assert tb >= N_SLOTS and h2.shape == (n * rpe, LANES) and npair % GROUP == 0
    g_rows = g_sel.reshape(-1, rpe, LANES)

    def g_index(i):
        r = i * tb
        return ((r // tt) * 2 + jnp.where(r % tt < ctx_len, 0, 1), 0, 0)

    return pl.pallas_call(
        functools.partial(_peer_kernel, tb=tb, npair=npair),
        out_shape=jax.ShapeDtypeStruct((n * rpe, LANES), F32),
        grid=(n // tb,),
        in_specs=[
            pl.BlockSpec((tb, npair), lambda i: (i, 0), memory_space=pltpu.SMEM),
            pl.BlockSpec((tb * rpe, LANES), lambda i: (i, 0)),
            pl.BlockSpec((tb, npair), lambda i: (i, 0)),
            pl.BlockSpec((tb * rpe, LANES), lambda i: (i, 0)),
            pl.BlockSpec((None, rpe, LANES), g_index),
            pl.BlockSpec(memory_space=pl.ANY),
        ],
        out_specs=pl.BlockSpec((tb * rpe, LANES), lambda i: (i, 0)),
        scratch_shapes=[pltpu.VMEM((N_SLOTS, npair * rpe, LANES), jnp.uint32),
                        pltpu.SemaphoreType.DMA((N_SLOTS,)),
                        pltpu.VMEM((npair, LANES), F32)],
        compiler_params=_cparams(("arbitrary",)),
        name="peer_experts",
    )(ex, h2, gate, x2, g_rows, table)


def _final_kernel(x_ref, w_ref, o_ref):
    x = x_ref[0]
    o_ref[0] = x * lax.rsqrt(jnp.mean(x * x, axis=-1, keepdims=True) + EPS) * w_ref[...]


def _final_norm(xa, w_row, ctx_len):
    b, tt, d = xa.shape
    t = tt - ctx_len
    tr = _pick(math.gcd(t, ctx_len), (256, 128))
    off = ctx_len // tr
    return pl.pallas_call(
        _final_kernel,
        out_shape=jax.ShapeDtypeStruct((b, t, d), F32),
        grid=(b, t // tr),
        in_specs=[pl.BlockSpec((1, tr, d), lambda bb, i: (bb, i + off, 0)),
                  pl.BlockSpec((1, d), lambda bb, i: (0, 0))],
        out_specs=pl.BlockSpec((1, tr, d), lambda bb, i: (bb, i, 0)),
        compiler_params=_cparams(("parallel", "parallel")),
        name="final_norm",
    )(xa, w_row)


def _rope_tables(t, ctx_len):
    tok = jnp.arange(t, dtype=I32)
    row = (tok // GRID_W).astype(F32)
    col = (tok % GRID_W).astype(F32)
    axis_dim = HEAD_DIM // 2
    inv = ROPE_THETA ** (-jnp.arange(0, axis_dim, 2, dtype=F32) / axis_dim)
    ang_r = row[:, None] * inv[None, :]
    ang_c = col[:, None] * inv[None, :]
    cos = jnp.concatenate([jnp.cos(ang_r), jnp.cos(ang_r), jnp.cos(ang_c), jnp.cos(ang_c)], axis=-1)
    sin = jnp.concatenate([-jnp.sin(ang_r), jnp.sin(ang_r), -jnp.sin(ang_c), jnp.sin(ang_c)], axis=-1)
    cos = jnp.concatenate([jnp.ones((ctx_len, HEAD_DIM), F32), cos], axis=0)
    sin = jnp.concatenate([jnp.zeros((ctx_len, HEAD_DIM), F32), sin], axis=0)
    return cos, sin


def _pack_experts(u, v):
    e, d = u.shape
    ub = lax.bitcast_convert_type(u.astype(BF16), jnp.uint16).astype(jnp.uint32)
    vb = lax.bitcast_convert_type(v.astype(BF16), jnp.uint16).astype(jnp.uint32)
    return (ub | (vb << 16)).reshape(e * (d // LANES), LANES)


def _mod_sel(mods_l, chunk, b, d):
    sl = slice(chunk * d, (chunk + 1) * d)
    ctx_row = jnp.broadcast_to(mods_l[b:b + 1, sl], (b, d))
    return jnp.stack([ctx_row, mods_l[:b, sl]], axis=1)


def kernel(x, c, ctx, c_ctx, ada_w, ada_b, norm1_w, norm2_w, ev_w_in, ev_w_out, ev_ret_decay, ev_q_norm, ev_k_norm, od_w_in, od_conv_w, od_conv_b, od_dt_bias, od_a_log, od_d, od_norm_w, od_w_out, peer_wq, peer_keys, peer_u, peer_v, final_norm_w):
    b, t, d = x.shape
    ctx_len = ctx.shape[1]
    depth = ada_w.shape[0]
    tt = ctx_len + t
    assert b + 1 <= 8 and ctx_len % CHUNK == 0 and t % CHUNK == 0 and t % GRID_W == 0
    assert d == ROWS_PER_EXPERT * LANES

    xa = jnp.concatenate([ctx, x], axis=1)
    crow = jnp.zeros((8, d), F32).at[:b].set(c).at[b].set(c_ctx)
    mods = _mods(crow, ada_w, ada_b)
    cos, sin = _rope_tables(t, ctx_len)

    ret_w = RET_HEADS * HEAD_DIM
    d_inner = od_w_out.shape[1]
    n_ssm_heads = d_inner // SSM_HEAD_DIM
    zx_cols = 2 * d_inner + 2 * N_GROUPS * D_STATE

    for layer in range(depth):
        m = mods[layer]
        sel = [_mod_sel(m, k, b, d) for k in range(N_MOD)]
        j = layer // 2
        nw1 = norm1_w[layer].reshape(1, d)
        if layer % 2 == 0:
            p = _proj(xa, nw1, sel[0], sel[1], ev_w_in[j].astype(BF16), BF16, ctx_len,
                      even_extras=(cos, sin, ev_q_norm[j].reshape(1, HEAD_DIM), ev_k_norm[j].reshape(1, HEAD_DIM)))
            lgb = jnp.broadcast_to(ev_ret_decay[j].astype(F32)[:, :, None, None], (2, RET_HEADS, 1, LANES))
            y_ret = _retention(p, lgb, ctx_len)
            y_att = _attention(p, ctx_len)
            w_out = ev_w_out[j].astype(BF16)
            xa = _outproj([y_ret, y_att], [w_out[:ret_w], w_out[ret_w:]], xa, sel[2], ctx_len)
        else:
            w_in = od_w_in[j]
            p = _proj(xa, nw1, sel[0], sel[1], w_in[:, :zx_cols].astype(BF16), BF16, ctx_len)
            dt_raw = _proj(xa, nw1, sel[0], sel[1], w_in[:, zx_cols:].astype(BF16), F32, ctx_len)
            xbc = _conv(p, d_inner, zx_cols - d_inner, od_conv_w[j], od_conv_b[j].reshape(1, -1), ctx_len)
            alog_b = jnp.broadcast_to(od_a_log[j].astype(F32).reshape(2 * n_ssm_heads, 1), (2 * n_ssm_heads, LANES))
            dskip_row = jnp.repeat(od_d[j].astype(F32), SSM_HEAD_DIM).reshape(1, d_inner)
            yn = _ssd(xbc, dt_raw, p, od_dt_bias[j].reshape(1, 2 * n_ssm_heads), alog_b, dskip_row,
                      od_norm_w[j].reshape(1, d_inner), ctx_len)
            xa = _outproj([yn], [od_w_out[j].astype(BF16)], xa, sel[2], ctx_len)

        q, h2 = _proj(xa, norm2_w[layer].reshape(1, d), sel[3], sel[4], peer_wq[layer].astype(BF16), F32,
                      ctx_len, emit_h=True)
        n = b * tt
        ex, gate = _route(q.reshape(n, -1), peer_keys[layer].astype(BF16))
        table = _pack_experts(peer_u[layer], peer_v[layer])
        tiles = (n * ROWS_PER_EXPERT, LANES)
        xa = _peer(ex, h2.reshape(tiles), gate, xa.reshape(tiles), sel[5], table, tt, ctx_len).reshape(b, tt, d)

    return _final_norm(xa, final_norm_w.reshape(1, d), ctx_len)
```

```python
import functools
import math

import jax
import jax.numpy as jnp
from jax import lax
from jax.experimental import pallas as pl
from jax.experimental.pallas import tpu as pltpu

F32 = jnp.float32
BF16 = jnp.bfloat16
I32 = jnp.int32

EPS = 1e-6
HEAD_DIM = 128
RET_HEADS = 8
ATT_HEADS = 8
ATT_KV_HEADS = 2
CHUNK = 128
GRID_W = 64
ROPE_THETA = 10000.0
SSM_HEAD_DIM = 64
D_STATE = 128
N_GROUPS = 8
D_CONV = 5
PEER_HEADS = 8
N_KEYS = 128
PEER_TOPK = 16
N_MOD = 6

LANES = 128
V7X_VMEM_BYTES = 64 * 1024 * 1024
VMEM_LIMIT = 52 * 1024 * 1024
NEG_BIG = -1e30

_NT = (((1,), (1,)), ((), ()))


def _pick(n, candidates):
    for c in candidates:
        if n % c == 0:
            return c
    raise ValueError(f"no tile for {n} in {candidates}")


def _cparams(sem, vmem=VMEM_LIMIT):
    return pltpu.CompilerParams(dimension_semantics=sem, vmem_limit_bytes=vmem)


def _sigmoid(x):
    return 1.0 / (1.0 + jnp.exp(-x))


def _silu(x):
    return x * _sigmoid(x)


def _softplus(x):
    return jnp.maximum(x, 0.0) + jnp.log(1.0 + jnp.exp(-jnp.abs(x)))


def _mods_kernel(c_ref, w_ref, b_ref, o_ref):
    sc = _silu(c_ref[...])
    acc = jnp.dot(sc.astype(BF16), w_ref[0].astype(BF16), preferred_element_type=F32)
    o_ref[0] = acc + b_ref[0]


def _mods(crow, ada_w, ada_b):
    depth, d, n = ada_w.shape
    tn = _pick(n, (1024, 512, 256, 128))
    return pl.pallas_call(
        _mods_kernel,
        out_shape=jax.ShapeDtypeStruct((depth, 8, n), F32),
        grid=(depth, n // tn),
        in_specs=[
            pl.BlockSpec((8, d), lambda l, j: (0, 0)),
            pl.BlockSpec((1, d, tn), lambda l, j: (l, 0, j)),
            pl.BlockSpec((1, 1, tn), lambda l, j: (l, 0, j)),
        ],
        out_specs=pl.BlockSpec((1, 8, tn), lambda l, j: (l, 0, j)),
        compiler_params=_cparams(("parallel", "parallel")),
        name="adaln_mods",
    )(crow, ada_w, ada_b.reshape(depth, 1, n))


def _norm_mod(x_ref, nw_ref, sh_ref, sc_ref, i, tm, ctx_len):
    x = x_ref[0]
    ms = jnp.mean(x * x, axis=-1, keepdims=True)
    y = x * lax.rsqrt(ms + EPS) * nw_ref[...]
    row = i * tm + lax.broadcasted_iota(I32, (tm, 1), 0)
    is_ctx = row < ctx_len
    sh = jnp.where(is_ctx, sh_ref[0, 0:1, :], sh_ref[0, 1:2, :])
    sc = jnp.where(is_ctx, sc_ref[0, 0:1, :], sc_ref[0, 1:2, :])
    return y * (1.0 + sc) + sh


def _rope(v, cos, sin_signed, lane_lo):
    rot = jnp.where(lane_lo, pltpu.roll(v, 96, 1), pltpu.roll(v, 32, 1))
    return v * cos + rot * sin_signed


def _head_rms(v, w):
    return v * lax.rsqrt(jnp.mean(v * v, axis=-1, keepdims=True) + EPS) * w


def _proj_plain_kernel(x_ref, nw_ref, sh_ref, sc_ref, w_ref, o_ref, *rest, tm, ctx_len, emit_h):
    if emit_h:
        h_out, h_scr = rest
    else:
        (h_scr,) = rest
    i = pl.program_id(1)
    j = pl.program_id(2)

    @pl.when(j == 0)
    def _():
        h = _norm_mod(x_ref, nw_ref, sh_ref, sc_ref, i, tm, ctx_len)
        h_scr[...] = h.astype(BF16)
        if emit_h:
            h_out[0] = h

    acc = jnp.dot(h_scr[...], w_ref[...], preferred_element_type=F32)
    o_ref[0] = acc.astype(o_ref.dtype)


def _proj_even_kernel(x_ref, nw_ref, sh_ref, sc_ref, w_ref, cos_ref, sin_ref, qn_ref, kn_ref,
                      o_ref, h_scr, *, tm, ctx_len, tn):
    i = pl.program_id(1)
    j = pl.program_id(2)

    @pl.when(j == 0)
    def _():
        h = _norm_mod(x_ref, nw_ref, sh_ref, sc_ref, i, tm, ctx_len)
        h_scr[...] = h.astype(BF16)

    acc = jnp.dot(h_scr[...], w_ref[...], preferred_element_type=F32)
    nh = tn // HEAD_DIM
    cos = cos_ref[...]
    sin = sin_ref[...]
    lane = lax.broadcasted_iota(I32, (1, HEAD_DIM), 1)
    lane_lo = (lane % 64) < 32
    k_scale = HEAD_DIM ** -0.5

    def heads(fn):
        for k in range(nh):
            sl = slice(k * HEAD_DIM, (k + 1) * HEAD_DIM)
            o_ref[0, :, sl] = fn(acc[:, sl], k).astype(o_ref.dtype)

    @pl.when(j < 2)
    def _():
        heads(lambda v, k: _rope(v, cos, sin, lane_lo))

    @pl.when((j >= 2) & (j < 4))
    def _():
        heads(lambda v, k: _rope(v, cos, sin, lane_lo) * k_scale)

    @pl.when((j >= 4) & (j < 8))
    def _():
        o_ref[0] = acc.astype(o_ref.dtype)

    @pl.when((j >= 8) & (j < 10))
    def _():
        heads(lambda v, k: _rope(_head_rms(v, qn_ref[...]), cos, sin, lane_lo))

    @pl.when(j == 10)
    def _():
        heads(lambda v, k: _rope(_head_rms(v, kn_ref[...]), cos, sin, lane_lo) if k < ATT_KV_HEADS else v)


def _proj(xa, nw, sh_sel, sc_sel, w, out_dtype, ctx_len, *, emit_h=False, even_extras=None):
    b, tt, d = xa.shape
    n = w.shape[1]
    tm = _pick(tt, (768, 384, 256, 128))
    tn = _pick(n, (512, 256, 128))
    grid = (b, tt // tm, n // tn)
    in_specs = [
        pl.BlockSpec((1, tm, d), lambda bb, i, j: (bb, i, 0)),
        pl.BlockSpec((1, d), lambda bb, i, j: (0, 0)),
        pl.BlockSpec((1, 2, d), lambda bb, i, j: (bb, 0, 0)),
        pl.BlockSpec((1, 2, d), lambda bb, i, j: (bb, 0, 0)),
        pl.BlockSpec((d, tn), lambda bb, i, j: (0, j)),
    ]
    args = [xa, nw, sh_sel, sc_sel, w]
    out_shape = jax.ShapeDtypeStruct((b, tt, n), out_dtype)
    out_specs = pl.BlockSpec((1, tm, tn), lambda bb, i, j: (bb, i, j))
    scratch = [pltpu.VMEM((tm, d), BF16)]
    if even_extras is not None:
        assert tn == 512 and n == 11 * 512
        cos, sin, qn, kn = even_extras
        in_specs += [
            pl.BlockSpec((tm, HEAD_DIM), lambda bb, i, j: (i, 0)),
            pl.BlockSpec((tm, HEAD_DIM), lambda bb, i, j: (i, 0)),
            pl.BlockSpec((1, HEAD_DIM), lambda bb, i, j: (0, 0)),
            pl.BlockSpec((1, HEAD_DIM), lambda bb, i, j: (0, 0)),
        ]
        args += [cos, sin, qn, kn]
        kern = functools.partial(_proj_even_kernel, tm=tm, ctx_len=ctx_len, tn=tn)
        name = "proj_even"
    else:
        kern = functools.partial(_proj_plain_kernel, tm=tm, ctx_len=ctx_len, emit_h=emit_h)
        name = "proj_plain"
        if emit_h:
            out_shape = (out_shape, jax.ShapeDtypeStruct((b, tt, d), F32))
            out_specs = (out_specs, pl.BlockSpec((1, tm, d), lambda bb, i, j: (bb, i, 0)))
    return pl.pallas_call(
        kern, out_shape=out_shape, grid=grid, in_specs=in_specs, out_specs=out_specs,
        scratch_shapes=scratch,
        compiler_params=_cparams(("parallel", "parallel", "arbitrary")),
        name=name,
    )(*args)


def _outproj_kernel(*refs, n_lhs, tm, ctx_len):
    lhs = refs[:n_lhs]
    ws = refs[n_lhs:2 * n_lhs]
    x_ref, g_ref, o_ref = refs[2 * n_lhs:]
    i = pl.program_id(1)
    acc = jnp.dot(lhs[0][0], ws[0][...], preferred_element_type=F32)
    for p in range(1, n_lhs):
        acc = acc + jnp.dot(lhs[p][0], ws[p][...], preferred_element_type=F32)
    row = i * tm + lax.broadcasted_iota(I32, (tm, 1), 0)
    g = jnp.where(row < ctx_len, g_ref[0, 0:1, :], g_ref[0, 1:2, :])
    o_ref[0] = x_ref[0] + g * acc


def _outproj(lhs_list, w_list, xa, g_sel, ctx_len):
    b, tt, d = xa.shape
    tm = _pick(tt, (768, 384, 256, 128))
    tn = _pick(d, (512, 256, 128))
    n_lhs = len(lhs_list)
    in_specs = []
    for l in lhs_list:
        in_specs.append(pl.BlockSpec((1, tm, l.shape[2]), lambda bb, i, j: (bb, i, 0)))
    for w in w_list:
        in_specs.append(pl.BlockSpec((w.shape[0], tn), lambda bb, i, j: (0, j)))
    in_specs += [
        pl.BlockSpec((1, tm, tn), lambda bb, i, j: (bb, i, j)),
        pl.BlockSpec((1, 2, tn), lambda bb, i, j: (bb, 0, j)),
    ]
    return pl.pallas_call(
        functools.partial(_outproj_kernel, n_lhs=n_lhs, tm=tm, ctx_len=ctx_len),
        out_shape=jax.ShapeDtypeStruct((b, tt, d), F32),
        grid=(b, tt // tm, d // tn),
        in_specs=in_specs,
        out_specs=pl.BlockSpec((1, tm, tn), lambda bb, i, j: (bb, i, j)),
        compiler_params=_cparams(("parallel", "parallel", "parallel")),
        name="outproj",
    )(*lhs_list, *w_list, xa, g_sel)


def _ret_kernel(*refs, direction, n_ctx_chunks):
    if direction == 0:
        q_ref, k_ref, v_ref, lg_ref, o_ref, st_scr, dm_scr, qd_scr, kd_scr = refs
    else:
        q_ref, k_ref, v_ref, lg_ref, of_ref, g_ref, o_ref, st_scr, dm_scr, qd_scr, kd_scr = refs
    c = pl.program_id(1)
    ii = lax.broadcasted_iota(I32, (CHUNK, CHUNK), 0).astype(F32)
    jj = lax.broadcasted_iota(I32, (CHUNK, CHUNK), 1).astype(F32)

    @pl.when(c == 0)
    def _():
        st_scr[...] = jnp.zeros_like(st_scr)
        for h in range(RET_HEADS):
            lg = -jnp.exp(lg_ref[h])
            if direction == 0:
                diff = ii - jj
                qe = ii + 1.0
                ke = (CHUNK - 1.0) - ii
            else:
                diff = jj - ii
                qe = CHUNK - ii
                ke = ii
            keep = diff >= 0
            dm_scr[h] = jnp.where(keep, jnp.exp(jnp.where(keep, diff, 0.0) * lg), 0.0)
            qd_scr[h] = jnp.exp(qe * lg)
            kd_scr[h] = jnp.exp(ke * lg)

    for h in range(RET_HEADS):
        sl = slice(h * HEAD_DIM, (h + 1) * HEAD_DIM)
        qh = q_ref[0, :, sl]
        kh = k_ref[0, :, sl]
        vh = v_ref[0, :, sl]
        st = st_scr[h]
        s = lax.dot_general(qh, kh, _NT, preferred_element_type=F32) * dm_scr[h]
        inner = jnp.dot(s.astype(BF16), vh, preferred_element_type=F32)
        cross = jnp.dot(qh, st.astype(BF16), preferred_element_type=F32) * qd_scr[h]
        out = inner + cross
        kdec = (kh.astype(F32) * kd_scr[h]).T.astype(BF16)
        cd = jnp.exp(CHUNK * (-jnp.exp(lg_ref[h])))
        st_scr[h] = cd * st + jnp.dot(kdec, vh, preferred_element_type=F32)
        if direction == 0:
            o_ref[0, :, sl] = out
        else:
            o = out + of_ref[0, :, sl]
            mu = jnp.mean(o, axis=-1, keepdims=True)
            var = jnp.mean(jnp.square(o - mu), axis=-1, keepdims=True)
            on = (o - mu) * lax.rsqrt(var + EPS)
            o_ref[0, :, sl] = (on * _silu(g_ref[0, :, sl].astype(F32))).astype(o_ref.dtype)


def _retention(p, lgb, ctx_len):
    b, tt, _ = p.shape
    nc = tt // CHUNK
    ncc = ctx_len // CHUNK
    w = RET_HEADS * HEAD_DIM

    def fwd_chunk(bb, c):
        return c

    def bwd_chunk(bb, c):
        return jnp.where(c < ncc, ncc - 1 - c, nc - 1 + ncc - c)

    def specs(chunk_fn):
        return [
            pl.BlockSpec((1, CHUNK, w), lambda bb, c: (bb, chunk_fn(bb, c), 0)),
            pl.BlockSpec((1, CHUNK, w), lambda bb, c: (bb, chunk_fn(bb, c), 1)),
            pl.BlockSpec((1, CHUNK, w), lambda bb, c: (bb, chunk_fn(bb, c), 2)),
        ]

    scratch = [pltpu.VMEM((RET_HEADS, HEAD_DIM, HEAD_DIM), F32) for _ in range(4)]
    lg_spec = lambda d: pl.BlockSpec((None, RET_HEADS, 1, LANES), lambda bb, c: (d, 0, 0, 0))
    o_f = pl.pallas_call(
        functools.partial(_ret_kernel, direction=0, n_ctx_chunks=ncc),
        out_shape=jax.ShapeDtypeStruct((b, tt, w), F32),
        grid=(b, nc),
        in_specs=specs(fwd_chunk) + [lg_spec(0)],
        out_specs=pl.BlockSpec((1, CHUNK, w), lambda bb, c: (bb, c, 0)),
        scratch_shapes=scratch,
        compiler_params=_cparams(("parallel", "arbitrary")),
        name="retention_fwd",
    )(p, p, p, lgb)
    y = pl.pallas_call(
        functools.partial(_ret_kernel, direction=1, n_ctx_chunks=ncc),
        out_shape=jax.ShapeDtypeStruct((b, tt, w), BF16),
        grid=(b, nc),
        in_specs=specs(bwd_chunk) + [
            lg_spec(1),
            pl.BlockSpec((1, CHUNK, w), lambda bb, c: (bb, bwd_chunk(bb, c), 0)),
            pl.BlockSpec((1, CHUNK, w), lambda bb, c: (bb, bwd_chunk(bb, c), 3)),
        ],
        out_specs=pl.BlockSpec((1, CHUNK, w), lambda bb, c: (bb, bwd_chunk(bb, c), 0)),
        scratch_shapes=scratch,
        compiler_params=_cparams(("parallel", "arbitrary")),
        name="retention_bwd",
    )(p, p, p, lgb, o_f, p)
    return y


def _attn_kernel(q_ref, k_ref, v_ref, o_ref, *, tq, ctx_len):
    i = pl.program_id(2)
    scale_log2e = (HEAD_DIM ** -0.5) * math.log2(math.e)

    def run(masked):
        s = lax.dot_general(q_ref[0], k_ref[0], _NT, preferred_element_type=F32) * scale_log2e
        if masked:
            qrow = i * tq + lax.broadcasted_iota(I32, (tq, 1), 0)
            key = lax.broadcasted_iota(I32, (1, s.shape[1]), 1)
            s = jnp.where((qrow < ctx_len) & (key >= ctx_len), NEG_BIG, s)
        pexp = jnp.exp2(s - jnp.max(s, axis=-1, keepdims=True))
        denom = jnp.sum(pexp, axis=-1, keepdims=True)
        out = jnp.dot(pexp.astype(BF16), v_ref[0], preferred_element_type=F32)
        o_ref[0] = (out / denom).astype(o_ref.dtype)

    has_ctx = i * tq < ctx_len

    @pl.when(has_ctx)
    def _():
        run(True)

    @pl.when(jnp.logical_not(has_ctx))
    def _():
        run(False)


def _attention(p, ctx_len):
    b, tt, _ = p.shape
    tq = _pick(tt, (256, 128))
    groups = ATT_HEADS // ATT_KV_HEADS
    q_base = 4 * RET_HEADS
    k_base = q_base + ATT_HEADS
    v_base = k_base + ATT_KV_HEADS
    return pl.pallas_call(
        functools.partial(_attn_kernel, tq=tq, ctx_len=ctx_len),
        out_shape=jax.ShapeDtypeStruct((b, tt, ATT_HEADS * HEAD_DIM), BF16),
        grid=(b, ATT_KV_HEADS, tt // tq, groups),
        in_specs=[
            pl.BlockSpec((1, tq, HEAD_DIM), lambda bb, kv, i, g: (bb, i, q_base + kv * groups + g)),
            pl.BlockSpec((1, tt, HEAD_DIM), lambda bb, kv, i, g: (bb, 0, k_base + kv)),
            pl.BlockSpec((1, tt, HEAD_DIM), lambda bb, kv, i, g: (bb, 0, v_base + kv)),
        ],
        out_specs=pl.BlockSpec((1, tq, HEAD_DIM), lambda bb, kv, i, g: (bb, i, kv * groups + g)),
        compiler_params=_cparams(("parallel", "parallel", "parallel", "arbitrary")),
        name="gqa_attention",
    )(p, p, p)


def _conv_kernel(main_ref, prev_ref, next_ref, w_ref, b_ref, o_ref, ext_scr, *, tr, halo, ctx_len, tt):
    i = pl.program_id(1)
    start = i * tr
    seg_lo = jnp.where(start < ctx_len, 0, ctx_len)
    seg_hi = jnp.where(start < ctx_len, ctx_len, tt)
    ext = jnp.concatenate([prev_ref[0], main_ref[0], next_ref[0]], axis=0).astype(F32)
    row = start - halo + lax.broadcasted_iota(I32, (tr + 2 * halo, 1), 0)
    ext_scr[...] = jnp.where((row >= seg_lo) & (row < seg_hi), ext, 0.0)
    pad = D_CONV // 2
    acc = b_ref[...] + w_ref[0:1, :] * ext_scr[pl.ds(halo - pad, tr), :]
    for k in range(1, D_CONV):
        acc = acc + w_ref[k:k + 1, :] * ext_scr[pl.ds(halo - pad + k, tr), :]
    o_ref[0] = _silu(acc).astype(o_ref.dtype)


def _conv(p, col0, width, conv_w, conv_b, ctx_len):
    b, tt, _ = p.shape
    tr = CHUNK
    halo = 16
    tc = _pick(math.gcd(width, col0), (2048, 1024, 512, 256, 128))
    assert ctx_len % tr == 0 and col0 % tc == 0
    cb0 = col0 // tc
    nblk = tt // halo
    r = tr // halo
    return pl.pallas_call(
        functools.partial(_conv_kernel, tr=tr, halo=halo, ctx_len=ctx_len, tt=tt),
        out_shape=jax.ShapeDtypeStruct((b, tt, width), BF16),
        grid=(b, tt // tr, width // tc),
        in_specs=[
            pl.BlockSpec((1, tr, tc), lambda bb, i, j: (bb, i, cb0 + j)),
            pl.BlockSpec((1, halo, tc), lambda bb, i, j: (bb, jnp.maximum(i * r - 1, 0), cb0 + j)),
            pl.BlockSpec((1, halo, tc), lambda bb, i, j: (bb, jnp.minimum((i + 1) * r, nblk - 1), cb0 + j)),
            pl.BlockSpec((D_CONV, tc), lambda bb, i, j: (0, j)),
            pl.BlockSpec((1, tc), lambda bb, i, j: (0, j)),
        ],
        out_specs=pl.BlockSpec((1, tr, tc), lambda bb, i, j: (bb, i, j)),
        scratch_shapes=[pltpu.VMEM((tr + 2 * halo, tc), F32)],
        compiler_params=_cparams(("parallel", "parallel", "parallel")),
        name="ssd_conv",
    )(p, p, p, conv_w, conv_b)


def _ssd_kernel(*refs, direction):
    if direction == 0:
        x_ref, b_ref, c_ref, dt_ref, bias_ref, alog_ref, o_ref, st_scr = refs
    else:
        (x_ref, b_ref, c_ref, dt_ref, bias_ref, alog_ref, yf_ref, z_ref, dsk_ref, nw_ref,
         o_ref, st_scr) = refs
    cstep = pl.program_id(1)

    @pl.when(cstep == 0)
    def _():
        st_scr[...] = jnp.zeros_like(st_scr)

    n_heads_dir = LANES // 2
    dt = _softplus(dt_ref[0] + bias_ref[...])
    dt_t = dt.T
    a_col = -jnp.exp(alog_ref[...])
    dta_t = dt_t * a_col
    kk = lax.broadcasted_iota(I32, (CHUNK, CHUNK), 0)
    jj = lax.broadcasted_iota(I32, (CHUNK, CHUNK), 1)
    if direction == 0:
        tri = (kk <= jj).astype(F32)
        keep = kk >= jj
        end_row = CHUNK - 1
    else:
        tri = (kk >= jj).astype(F32)
        keep = kk <= jj
        end_row = 0
    cs_t = jnp.dot(dta_t, tri, preferred_element_type=F32, precision=lax.Precision.HIGHEST)
    cs = cs_t.T
    lane = lax.broadcasted_iota(I32, (1, LANES), 1)
    lo = lane < SSM_HEAD_DIM
    hpg = n_heads_dir // N_GROUPS

    for g in range(N_GROUPS):
        gs = slice(g * D_STATE, (g + 1) * D_STATE)
        cmat = c_ref[0, :, gs]
        bmat = b_ref[0, :, gs]
        b_t = bmat.astype(F32).T.astype(BF16)
        cb = jnp.dot(cmat, b_t, preferred_element_type=F32)
        ysq = jnp.zeros((CHUNK, 1), F32)
        ypairs = []
        for pr in range(hpg // 2):
            h0 = g * hpg + 2 * pr
            hd0 = direction * n_heads_dir + h0
            xs = slice(h0 * SSM_HEAD_DIM, (h0 + 2) * SSM_HEAD_DIM)
            xpair = x_ref[0, :, xs]
            cs_b = []
            dt_b = []
            ydiag = []
            for u in range(2):
                hd = hd0 + u
                cs_col = jnp.broadcast_to(cs[:, hd:hd + 1], (CHUNK, LANES))
                dt_col = jnp.broadcast_to(dt[:, hd:hd + 1], (CHUNK, LANES))
                seg = cs_col - cs_t[hd:hd + 1, :]
                lm = jnp.where(keep, jnp.exp(jnp.where(keep, seg, 0.0)), 0.0)
                wmat = cb * lm * dt_t[hd:hd + 1, :]
                ydiag.append(jnp.dot(wmat.astype(BF16), xpair, preferred_element_type=F32))
                cs_b.append(cs_col)
                dt_b.append(dt_col)
            cs_pair = jnp.where(lo, cs_b[0], cs_b[1])
            dt_pair = jnp.where(lo, dt_b[0], dt_b[1])
            e_pair = jnp.exp(cs_pair)
            tot = cs_pair[end_row:end_row + 1, :]
            st = st_scr[g, pr]
            y_off = jnp.dot(cmat, st.astype(BF16), preferred_element_type=F32) * e_pair
            y = jnp.where(lo, ydiag[0], ydiag[1]) + y_off
            to_end = jnp.exp(tot - cs_pair) * dt_pair
            xw = (xpair.astype(F32) * to_end).astype(BF16)
            st_scr[g, pr] = jnp.exp(tot) * st + jnp.dot(b_t, xw, preferred_element_type=F32)
            if direction == 0:
                o_ref[0, :, xs] = y
            else:
                y = y + yf_ref[0, :, xs] + dsk_ref[:, xs] * xpair.astype(F32)
                y = y * _silu(z_ref[0, :, xs].astype(F32))
                ysq = ysq + jnp.sum(y * y, axis=-1, keepdims=True)
                ypairs.append((xs, y))
        if direction == 1:
            inv = lax.rsqrt(ysq / (hpg * SSM_HEAD_DIM) + EPS)
            for xs, y in ypairs:
                o_ref[0, :, xs] = (y * inv * nw_ref[:, xs]).astype(o_ref.dtype)


def _ssd(xbc, dt_raw, p, dt_bias_row, alog_b, dskip_row, norm_w_row, ctx_len):
    b, tt, _ = xbc.shape
    nc = tt // CHUNK
    ncc = ctx_len // CHUNK
    d_inner = dskip_row.shape[1]
    gw = N_GROUPS * D_STATE
    nb = d_inner // gw

    def fwd_chunk(c):
        return c

    def bwd_chunk(c):
        return jnp.where(c < ncc, ncc - 1 - c, nc - 1 + ncc - c)

    def specs(cf):
        return [
            pl.BlockSpec((1, CHUNK, d_inner), lambda bb, c: (bb, cf(c), 0)),
            pl.BlockSpec((1, CHUNK, gw), lambda bb, c: (bb, cf(c), nb)),
            pl.BlockSpec((1, CHUNK, gw), lambda bb, c: (bb, cf(c), nb + 1)),
            pl.BlockSpec((1, CHUNK, LANES), lambda bb, c: (bb, cf(c), 0)),
            pl.BlockSpec((1, LANES), lambda bb, c: (0, 0)),
            pl.BlockSpec((LANES, LANES), lambda bb, c: (0, 0)),
        ]

    hpg = (d_inner // SSM_HEAD_DIM) // N_GROUPS
    scratch = [pltpu.VMEM((N_GROUPS, hpg // 2, D_STATE, 2 * SSM_HEAD_DIM), F32)]
    y_f = pl.pallas_call(
        functools.partial(_ssd_kernel, direction=0),
        out_shape=jax.ShapeDtypeStruct((b, tt, d_inner), F32),
        grid=(b, nc),
        in_specs=specs(fwd_chunk),
        out_specs=pl.BlockSpec((1, CHUNK, d_inner), lambda bb, c: (bb, c, 0)),
        scratch_shapes=scratch,
        compiler_params=_cparams(("parallel", "arbitrary")),
        name="ssd_fwd",
    )(xbc, xbc, xbc, dt_raw, dt_bias_row, alog_b)
    yn = pl.pallas_call(
        functools.partial(_ssd_kernel, direction=1),
        out_shape=jax.ShapeDtypeStruct((b, tt, d_inner), BF16),
        grid=(b, nc),
        in_specs=specs(bwd_chunk) + [
            pl.BlockSpec((1, CHUNK, d_inner), lambda bb, c: (bb, bwd_chunk(c), 0)),
            pl.BlockSpec((1, CHUNK, d_inner), lambda bb, c: (bb, bwd_chunk(c), 0)),
            pl.BlockSpec((1, d_inner), lambda bb, c: (0, 0)),
            pl.BlockSpec((1, d_inner), lambda bb, c: (0, 0)),
        ],
        out_specs=pl.BlockSpec((1, CHUNK, d_inner), lambda bb, c: (bb, bwd_chunk(c), 0)),
        scratch_shapes=scratch,
        compiler_params=_cparams(("parallel", "arbitrary")),
        name="ssd_bwd",
    )(xbc, xbc, xbc, dt_raw, dt_bias_row, alog_b, y_f, p, dskip_row, norm_w_row)
    return yn


def _topk_rows(sc, n_rows, k):
    rid = lax.broadcasted_iota(I32, sc.shape, 0)
    vals = []
    idxs = []
    for _ in range(k):
        m = jnp.max(sc, axis=0, keepdims=True)
        idx = jnp.min(jnp.where(sc == m, rid, n_rows), axis=0, keepdims=True)
        vals.append(m)
        idxs.append(idx)
        sc = jnp.where(rid == idx, -jnp.inf, sc)
    return jnp.concatenate(vals, axis=0), jnp.concatenate(idxs, axis=0), rid


def _route_kernel(q_ref, keys_ref, ex_ref, gate_ref, ext_scr, gt_scr):
    def head_body(h, carry):
        side = []
        for s in range(2):
            col = pl.multiple_of((h * 2 + s) * HEAD_DIM, HEAD_DIM)
            qs = q_ref[:, pl.ds(col, HEAD_DIM)].astype(BF16)
            ky = keys_ref[h, s]
            sc = lax.dot_general(ky, qs, _NT, preferred_element_type=F32)
            v, ix, _ = _topk_rows(sc, N_KEYS, PEER_TOPK)
            side.append((v, ix))
        (s1, i1), (s2, i2) = side
        r8 = lax.broadcasted_iota(I32, (8, 1), 0)
        blocks = []
        for half in range(2):
            rows = slice(8 * half, 8 * half + 8)
            blocks.append((s1[0:1, :] + s2[rows, :], i1[0:1, :] * N_KEYS + i2[rows, :], 8 * half + r8))
        for a in range(1, 8):
            nb = PEER_TOPK // (a + 1)
            val = s1[a:a + 1, :] + s2[0:8, :]
            if nb < 8:
                val = jnp.where(r8 < nb, val, -jnp.inf)
            blocks.append((val, i1[a:a + 1, :] * N_KEYS + i2[0:8, :], PEER_TOPK * a + r8))
        blocks.append((s1[8:16, :] + s2[0:1, :], i1[8:16, :] * N_KEYS + i2[0:1, :],
                       PEER_TOPK * (8 + r8)))
        cand = jnp.concatenate([blk[0] for blk in blocks], axis=0)
        cidx = jnp.concatenate([blk[1] for blk in blocks], axis=0)
        flat = jnp.concatenate([blk[2] for blk in blocks], axis=0)
        best = []
        exps = []
        for _ in range(PEER_TOPK):
            m = jnp.max(cand, axis=0, keepdims=True)
            pos = jnp.min(jnp.where(cand == m, flat, PEER_TOPK * PEER_TOPK), axis=0, keepdims=True)
            hit = flat == pos
            exps.append(jnp.max(jnp.where(hit, cidx, -1), axis=0, keepdims=True))
            best.append(m)
            cand = jnp.where(hit, -jnp.inf, cand)
        best = jnp.concatenate(best, axis=0)
        e = jnp.exp(best - best[0:1, :])
        gate = e / jnp.sum(e, axis=0, keepdims=True)
        row0 = pl.multiple_of(h * PEER_TOPK, PEER_TOPK)
        ext_scr[pl.ds(row0, PEER_TOPK), :] = jnp.concatenate(exps, axis=0)
        gt_scr[pl.ds(row0, PEER_TOPK), :] = gate
        return carry

    lax.fori_loop(0, PEER_HEADS, head_body, 0, unroll=4)
    ex_ref[...] = ext_scr[...].T
    gate_ref[...] = gt_scr[...].T


def _route(q, keys_bf16):
    n, dq = q.shape
    tl = LANES
    npair = PEER_HEADS * PEER_TOPK
    return pl.pallas_call(
        _route_kernel,
        out_shape=(jax.ShapeDtypeStruct((n, npair), I32), jax.ShapeDtypeStruct((n, npair), F32)),
        grid=(n // tl,),
        in_specs=[
            pl.BlockSpec((tl, dq), lambda i: (i, 0)),
            pl.BlockSpec(keys_bf16.shape, lambda i: (0, 0, 0, 0)),
        ],
        out_specs=(pl.BlockSpec((tl, npair), lambda i: (i, 0)), pl.BlockSpec((tl, npair), lambda i: (i, 0))),
        scratch_shapes=[pltpu.VMEM((npair, tl), I32), pltpu.VMEM((npair, tl), F32)],
        compiler_params=_cparams(("parallel",)),
        name="peer_route",
    )(q, keys_bf16)


ROWS_PER_EXPERT = 16
N_SLOTS = 10
LOOKAHEAD = 8
TOKENS_PER_STEP = 2
GROUP = 8


def _peer_kernel(ex_ref, h_ref, gate_ref, x_ref, gmod_ref, tab_ref, o_ref, buf, sems, w_scr, *, tb, npair):
    rpe = ROWS_PER_EXPERT
    rows = npair * rpe
    n_groups = npair // GROUP
    sub = lax.broadcasted_iota(I32, (8, LANES), 0)
    lane = lax.broadcasted_iota(I32, (8, LANES), 1)
    masks = [(sub & sh) == 0 for sh in (1, 2, 4)]
    hi_mask = jnp.uint32(0xFFFF0000)

    def wait_slot(slot):
        pltpu.make_async_copy(tab_ref.at[pl.ds(0, rows), :], buf.at[slot], sems.at[slot]).wait()

    def issue(t, slot, j0, j1):
        for j in range(j0, j1):
            e = ex_ref[t, j]
            src = tab_ref.at[pl.ds(pl.multiple_of(e * rpe, rpe), rpe), :]
            pltpu.make_async_copy(src, buf.at[slot, pl.ds(j * rpe, rpe), :], sems.at[slot]).start(priority=j % 2)

    def combine(a, b, level):
        m = masks[level]
        return jnp.where(m, a, b) + pltpu.roll(jnp.where(m, b, a), 1 << level, 0)

    def load_token(t):
        r0 = pl.multiple_of(t * rpe, rpe)
        h0 = h_ref[pl.ds(r0, 8), :]
        h1 = h_ref[pl.ds(r0 + 8, 8), :]
        grow = jnp.broadcast_to(gate_ref[pl.ds(t, 1), :], (8, LANES))
        return r0, h0, h1, grow

    def group_step(t, tok, g, acc, u):
        _, h0, h1, grow = tok
        slot = t % N_SLOTS
        acc0, acc1 = acc
        parts = []
        words = []
        for k in range(GROUP):
            base = (g * GROUP + k) * rpe
            x0 = buf[slot, pl.ds(base, 8), :]
            x1 = buf[slot, pl.ds(base + 8, 8), :]
            words.append((x0, x1))
            parts.append(pltpu.bitcast(x0 << 16, F32) * h0 + pltpu.bitcast(x1 << 16, F32) * h1)
        for level in range(3):
            parts = [combine(parts[2 * i], parts[2 * i + 1], level) for i in range(len(parts) // 2)]
        act = jnp.sum(parts[0], axis=-1, keepdims=True)
        gcol = jnp.sum(jnp.where(lane == g * GROUP + sub, grow, 0.0), axis=-1, keepdims=True)
        wcol = 0.5 * act * (1.0 + lax.erf(act * (2.0 ** -0.5))) * gcol
        w_scr[u, pl.ds(g * GROUP, GROUP), :] = jnp.broadcast_to(wcol, (GROUP, LANES))
        for k in range(GROUP):
            wb = jnp.broadcast_to(w_scr[u, pl.ds(g * GROUP + k, 1), :], (8, LANES))
            acc0 = acc0 + wb * pltpu.bitcast(words[k][0] & hi_mask, F32)
            acc1 = acc1 + wb * pltpu.bitcast(words[k][1] & hi_mask, F32)
        return acc0, acc1

    def store_token(tok, acc):
        r0 = tok[0]
        o_ref[pl.ds(r0, 8), :] = x_ref[pl.ds(r0, 8), :] + gmod_ref[0:8, :] * acc[0]
        o_ref[pl.ds(r0 + 8, 8), :] = x_ref[pl.ds(r0 + 8, 8), :] + gmod_ref[8:16, :] * acc[1]

    def tokens(t_first, refill):
        ts = [t_first + u for u in range(TOKENS_PER_STEP)]
        for t in ts:
            wait_slot(t % N_SLOTS)
        toks = [load_token(t) for t in ts]
        accs = [(jnp.zeros((8, LANES), F32), jnp.zeros((8, LANES), F32)) for _ in ts]
        for g in range(n_groups):
            if refill:
                for t in ts:
                    issue(t + LOOKAHEAD, (t + LOOKAHEAD) % N_SLOTS, g * GROUP, (g + 1) * GROUP)
            accs = [group_step(t, tok, g, acc, u) for u, (t, tok, acc) in enumerate(zip(ts, toks, accs))]
        for tok, acc in zip(toks, accs):
            store_token(tok, acc)

    for t0 in range(LOOKAHEAD):
        issue(t0, t0, 0, npair)

    def main_body(it, carry):
        tokens(it * TOKENS_PER_STEP, True)
        return carry

    def tail_body(it, carry):
        tokens(it * TOKENS_PER_STEP, False)
        return carry

    n_main = (tb - LOOKAHEAD) // TOKENS_PER_STEP
    lax.fori_loop(0, n_main, main_body, 0)
    lax.fori_loop(n_main, tb // TOKENS_PER_STEP, tail_body, 0)


def _peer(ex, h2, gate, x2, g_sel, table, tt, ctx_len):
    n, npair = ex.shape
    rpe = ROWS_PER_EXPERT
    tb = _pick(math.gcd(tt, ctx_len), (128, 64, 32, 16, 8))
    assert tb >= N_SLOTS and h2.shape == (n * rpe, LANES) and npair % GROUP == 0
    assert (tb - LOOKAHEAD) % TOKENS_PER_STEP == 0 and tb % TOKENS_PER_STEP == 0
    assert N_SLOTS - LOOKAHEAD >= TOKENS_PER_STEP
    g_rows = g_sel.reshape(-1, rpe, LANES)

    def g_index(i):
        r = i * tb
        return ((r // tt) * 2 + jnp.where(r % tt < ctx_len, 0, 1), 0, 0)

    return pl.pallas_call(
        functools.partial(_peer_kernel, tb=tb, npair=npair),
        out_shape=jax.ShapeDtypeStruct((n * rpe, LANES), F32),
        grid=(n // tb,),
        in_specs=[
            pl.BlockSpec((tb, npair), lambda i: (i, 0), memory_space=pltpu.SMEM),
            pl.BlockSpec((tb * rpe, LANES), lambda i: (i, 0)),
            pl.BlockSpec((tb, npair), lambda i: (i, 0)),
            pl.BlockSpec((tb * rpe, LANES), lambda i: (i, 0)),
            pl.BlockSpec((None, rpe, LANES), g_index),
            pl.BlockSpec(memory_space=pl.ANY),
        ],
        out_specs=pl.BlockSpec((tb * rpe, LANES), lambda i: (i, 0)),
        scratch_shapes=[pltpu.VMEM((N_SLOTS, npair * rpe, LANES), jnp.uint32),
                        pltpu.SemaphoreType.DMA((N_SLOTS,)),
                        pltpu.VMEM((TOKENS_PER_STEP, npair, LANES), F32)],
        compiler_params=_cparams(("arbitrary",)),
        name="peer_experts",
    )(ex, h2, gate, x2, g_rows, table)


def _final_kernel(x_ref, w_ref, o_ref):
    x = x_ref[0]
    o_ref[0] = x * lax.rsqrt(jnp.mean(x * x, axis=-1, keepdims=True) + EPS) * w_ref[...]


def _final_norm(xa, w_row, ctx_len):
    b, tt, d = xa.shape
    t = tt - ctx_len
    tr = _pick(math.gcd(t, ctx_len), (256, 128))
    off = ctx_len // tr
    return pl.pallas_call(
        _final_kernel,
        out_shape=jax.ShapeDtypeStruct((b, t, d), F32),
        grid=(b, t // tr),
        in_specs=[pl.BlockSpec((1, tr, d), lambda bb, i: (bb, i + off, 0)),
                  pl.BlockSpec((1, d), lambda bb, i: (0, 0))],
        out_specs=pl.BlockSpec((1, tr, d), lambda bb, i: (bb, i, 0)),
        compiler_params=_cparams(("parallel", "parallel")),
        name="final_norm",
    )(xa, w_row)


def _rope_tables(t, ctx_len):
    tok = jnp.arange(t, dtype=I32)
    row = (tok // GRID_W).astype(F32)
    col = (tok % GRID_W).astype(F32)
    axis_dim = HEAD_DIM // 2
    inv = ROPE_THETA ** (-jnp.arange(0, axis_dim, 2, dtype=F32) / axis_dim)
    ang_r = row[:, None] * inv[None, :]
    ang_c = col[:, None] * inv[None, :]
    cos = jnp.concatenate([jnp.cos(ang_r), jnp.cos(ang_r), jnp.cos(ang_c), jnp.cos(ang_c)], axis=-1)
    sin = jnp.concatenate([-jnp.sin(ang_r), jnp.sin(ang_r), -jnp.sin(ang_c), jnp.sin(ang_c)], axis=-1)
    cos = jnp.concatenate([jnp.ones((ctx_len, HEAD_DIM), F32), cos], axis=0)
    sin = jnp.concatenate([jnp.zeros((ctx_len, HEAD_DIM), F32), sin], axis=0)
    return cos, sin


def _pack_experts(u, v):
    e, d = u.shape
    ub = lax.bitcast_convert_type(u.astype(BF16), jnp.uint16).astype(jnp.uint32)
    vb = lax.bitcast_convert_type(v.astype(BF16), jnp.uint16).astype(jnp.uint32)
    return (ub | (vb << 16)).reshape(e * (d // LANES), LANES)


def _mod_sel(mods_l, chunk, b, d):
    sl = slice(chunk * d, (chunk + 1) * d)
    ctx_row = jnp.broadcast_to(mods_l[b:b + 1, sl], (b, d))
    return jnp.stack([ctx_row, mods_l[:b, sl]], axis=1)


def kernel(x, c, ctx, c_ctx, ada_w, ada_b, norm1_w, norm2_w, ev_w_in, ev_w_out, ev_ret_decay, ev_q_norm, ev_k_norm, od_w_in, od_conv_w, od_conv_b, od_dt_bias, od_a_log, od_d, od_norm_w, od_w_out, peer_wq, peer_keys, peer_u, peer_v, final_norm_w):
    b, t, d = x.shape
    ctx_len = ctx.shape[1]
    depth = ada_w.shape[0]
    tt = ctx_len + t
    assert b + 1 <= 8 and ctx_len % CHUNK == 0 and t % CHUNK == 0 and t % GRID_W == 0
    assert d == ROWS_PER_EXPERT * LANES

    xa = jnp.concatenate([ctx, x], axis=1)
    crow = jnp.zeros((8, d), F32).at[:b].set(c).at[b].set(c_ctx)
    mods = _mods(crow, ada_w, ada_b)
    cos, sin = _rope_tables(t, ctx_len)

    ret_w = RET_HEADS * HEAD_DIM
    d_inner = od_w_out.shape[1]
    n_ssm_heads = d_inner // SSM_HEAD_DIM
    zx_cols = 2 * d_inner + 2 * N_GROUPS * D_STATE

    for layer in range(depth):
        m = mods[layer]
        sel = [_mod_sel(m, k, b, d) for k in range(N_MOD)]
        j = layer // 2
        nw1 = norm1_w[layer].reshape(1, d)
        if layer % 2 == 0:
            p = _proj(xa, nw1, sel[0], sel[1], ev_w_in[j].astype(BF16), BF16, ctx_len,
                      even_extras=(cos, sin, ev_q_norm[j].reshape(1, HEAD_DIM), ev_k_norm[j].reshape(1, HEAD_DIM)))
            lgb = jnp.broadcast_to(ev_ret_decay[j].astype(F32)[:, :, None, None], (2, RET_HEADS, 1, LANES))
            y_ret = _retention(p, lgb, ctx_len)
            y_att = _attention(p, ctx_len)
            w_out = ev_w_out[j].astype(BF16)
            xa = _outproj([y_ret, y_att], [w_out[:ret_w], w_out[ret_w:]], xa, sel[2], ctx_len)
        else:
            w_in = od_w_in[j]
            p = _proj(xa, nw1, sel[0], sel[1], w_in[:, :zx_cols].astype(BF16), BF16, ctx_len)
            dt_raw = _proj(xa, nw1, sel[0], sel[1], w_in[:, zx_cols:].astype(BF16), F32, ctx_len)
            xbc = _conv(p, d_inner, zx_cols - d_inner, od_conv_w[j], od_conv_b[j].reshape(1, -1), ctx_len)
            alog_b = jnp.broadcast_to(od_a_log[j].astype(F32).reshape(2 * n_ssm_heads, 1), (2 * n_ssm_heads, LANES))
            dskip_row = jnp.repeat(od_d[j].astype(F32), SSM_HEAD_DIM).reshape(1, d_inner)
            yn = _ssd(xbc, dt_raw, p, od_dt_bias[j].reshape(1, 2 * n_ssm_heads), alog_b, dskip_row,
                      od_norm_w[j].reshape(1, d_inner), ctx_len)
            xa = _outproj([yn], [od_w_out[j].astype(BF16)], xa, sel[2], ctx_len)

        q, h2 = _proj(xa, norm2_w[layer].reshape(1, d), sel[3], sel[4], peer_wq[layer].astype(BF16), F32,
                      ctx_len, emit_h=True)
        n = b * tt
        ex, gate = _route(q.reshape(n, -1), peer_keys[layer].astype(BF16))
        table = _pack_experts(peer_u[layer], peer_v[layer])
        tiles = (n * ROWS_PER_EXPERT, LANES)
        xa = _peer(ex, h2.reshape(tiles), gate, xa.reshape(tiles), sel[5], table, tt, ctx_len).reshape(b, tt, d)

    return _final_norm(xa, final_norm_w.reshape(1, d), ctx_len)
```

```python
import functools
import math

import jax
import jax.numpy as jnp
from jax import lax
from jax.experimental import pallas as pl
from jax.experimental.pallas import tpu as pltpu

F32 = jnp.float32
BF16 = jnp.bfloat16
I32 = jnp.int32

EPS = 1e-6
HEAD_DIM = 128
RET_HEADS = 8
ATT_HEADS = 8
ATT_KV_HEADS = 2
CHUNK = 128
GRID_W = 64
ROPE_THETA = 10000.0
SSM_HEAD_DIM = 64
D_STATE = 128
N_GROUPS = 8
D_CONV = 5
PEER_HEADS = 8
N_KEYS = 128
PEER_TOPK = 16
N_MOD = 6

LANES = 128
V7X_VMEM_BYTES = 64 * 1024 * 1024
VMEM_LIMIT = 52 * 1024 * 1024
NEG_BIG = -1e30

_NT = (((1,), (1,)), ((), ()))


def _pick(n, candidates):
    for c in candidates:
        if n % c == 0:
            return c
    raise ValueError(f"no tile for {n} in {candidates}")


def _cparams(sem, vmem=VMEM_LIMIT):
    return pltpu.CompilerParams(dimension_semantics=sem, vmem_limit_bytes=vmem)


def _sigmoid(x):
    return 1.0 / (1.0 + jnp.exp(-x))


def _silu(x):
    return x * _sigmoid(x)


def _softplus(x):
    return jnp.maximum(x, 0.0) + jnp.log(1.0 + jnp.exp(-jnp.abs(x)))


def _mods_kernel(c_ref, w_ref, b_ref, o_ref):
    sc = _silu(c_ref[...])
    acc = jnp.dot(sc.astype(BF16), w_ref[0].astype(BF16), preferred_element_type=F32)
    o_ref[0] = acc + b_ref[0]


def _mods(crow, ada_w, ada_b):
    depth, d, n = ada_w.shape
    tn = _pick(n, (1024, 512, 256, 128))
    return pl.pallas_call(
        _mods_kernel,
        out_shape=jax.ShapeDtypeStruct((depth, 8, n), F32),
        grid=(depth, n // tn),
        in_specs=[
            pl.BlockSpec((8, d), lambda l, j: (0, 0)),
            pl.BlockSpec((1, d, tn), lambda l, j: (l, 0, j)),
            pl.BlockSpec((1, 1, tn), lambda l, j: (l, 0, j)),
        ],
        out_specs=pl.BlockSpec((1, 8, tn), lambda l, j: (l, 0, j)),
        compiler_params=_cparams(("parallel", "parallel")),
        name="adaln_mods",
    )(crow, ada_w, ada_b.reshape(depth, 1, n))


def _norm_mod(x_ref, nw_ref, sh_ref, sc_ref, i, tm, ctx_len):
    x = x_ref[0]
    ms = jnp.mean(x * x, axis=-1, keepdims=True)
    y = x * lax.rsqrt(ms + EPS) * nw_ref[...]
    row = i * tm + lax.broadcasted_iota(I32, (tm, 1), 0)
    is_ctx = row < ctx_len
    sh = jnp.where(is_ctx, sh_ref[0, 0:1, :], sh_ref[0, 1:2, :])
    sc = jnp.where(is_ctx, sc_ref[0, 0:1, :], sc_ref[0, 1:2, :])
    return y * (1.0 + sc) + sh


def _rope(v, cos, sin_signed, lane_lo):
    rot = jnp.where(lane_lo, pltpu.roll(v, 96, 1), pltpu.roll(v, 32, 1))
    return v * cos + rot * sin_signed


def _head_rms(v, w):
    return v * lax.rsqrt(jnp.mean(v * v, axis=-1, keepdims=True) + EPS) * w


def _proj_plain_kernel(x_ref, nw_ref, sh_ref, sc_ref, w_ref, o_ref, *rest, tm, ctx_len, emit_h):
    if emit_h:
        h_out, h_scr = rest
    else:
        (h_scr,) = rest
    i = pl.program_id(1)
    j = pl.program_id(2)

    @pl.when(j == 0)
    def _():
        h = _norm_mod(x_ref, nw_ref, sh_ref, sc_ref, i, tm, ctx_len)
        h_scr[...] = h.astype(BF16)
        if emit_h:
            h_out[0] = h

    acc = jnp.dot(h_scr[...], w_ref[...], preferred_element_type=F32)
    o_ref[0] = acc.astype(o_ref.dtype)


def _proj_even_kernel(x_ref, nw_ref, sh_ref, sc_ref, w_ref, cos_ref, sin_ref, qn_ref, kn_ref,
                      o_ref, h_scr, *, tm, ctx_len, tn):
    i = pl.program_id(1)
    j = pl.program_id(2)

    @pl.when(j == 0)
    def _():
        h = _norm_mod(x_ref, nw_ref, sh_ref, sc_ref, i, tm, ctx_len)
        h_scr[...] = h.astype(BF16)

    acc = jnp.dot(h_scr[...], w_ref[...], preferred_element_type=F32)
    nh = tn // HEAD_DIM
    cos = cos_ref[...]
    sin = sin_ref[...]
    lane = lax.broadcasted_iota(I32, (1, HEAD_DIM), 1)
    lane_lo = (lane % 64) < 32
    k_scale = HEAD_DIM ** -0.5

    def heads(fn):
        for k in range(nh):
            sl = slice(k * HEAD_DIM, (k + 1) * HEAD_DIM)
            o_ref[0, :, sl] = fn(acc[:, sl], k).astype(o_ref.dtype)

    @pl.when(j < 2)
    def _():
        heads(lambda v, k: _rope(v, cos, sin, lane_lo))

    @pl.when((j >= 2) & (j < 4))
    def _():
        heads(lambda v, k: _rope(v, cos, sin, lane_lo) * k_scale)

    @pl.when((j >= 4) & (j < 8))
    def _():
        o_ref[0] = acc.astype(o_ref.dtype)

    @pl.when((j >= 8) & (j < 10))
    def _():
        heads(lambda v, k: _rope(_head_rms(v, qn_ref[...]), cos, sin, lane_lo))

    @pl.when(j == 10)
    def _():
        heads(lambda v, k: _rope(_head_rms(v, kn_ref[...]), cos, sin, lane_lo) if k < ATT_KV_HEADS else v)


def _proj(xa, nw, sh_sel, sc_sel, w, out_dtype, ctx_len, *, emit_h=False, even_extras=None):
    b, tt, d = xa.shape
    n = w.shape[1]
    tm = _pick(tt, (768, 384, 256, 128))
    tn = _pick(n, (512, 256, 128))
    grid = (b, tt // tm, n // tn)
    in_specs = [
        pl.BlockSpec((1, tm, d), lambda bb, i, j: (bb, i, 0)),
        pl.BlockSpec((1, d), lambda bb, i, j: (0, 0)),
        pl.BlockSpec((1, 2, d), lambda bb, i, j: (bb, 0, 0)),
        pl.BlockSpec((1, 2, d), lambda bb, i, j: (bb, 0, 0)),
        pl.BlockSpec((d, tn), lambda bb, i, j: (0, j)),
    ]
    args = [xa, nw, sh_sel, sc_sel, w]
    out_shape = jax.ShapeDtypeStruct((b, tt, n), out_dtype)
    out_specs = pl.BlockSpec((1, tm, tn), lambda bb, i, j: (bb, i, j))
    scratch = [pltpu.VMEM((tm, d), BF16)]
    if even_extras is not None:
        assert tn == 512 and n == 11 * 512
        cos, sin, qn, kn = even_extras
        in_specs += [
            pl.BlockSpec((tm, HEAD_DIM), lambda bb, i, j: (i, 0)),
            pl.BlockSpec((tm, HEAD_DIM), lambda bb, i, j: (i, 0)),
            pl.BlockSpec((1, HEAD_DIM), lambda bb, i, j: (0, 0)),
            pl.BlockSpec((1, HEAD_DIM), lambda bb, i, j: (0, 0)),
        ]
        args += [cos, sin, qn, kn]
        kern = functools.partial(_proj_even_kernel, tm=tm, ctx_len=ctx_len, tn=tn)
        name = "proj_even"
    else:
        kern = functools.partial(_proj_plain_kernel, tm=tm, ctx_len=ctx_len, emit_h=emit_h)
        name = "proj_plain"
        if emit_h:
            out_shape = (out_shape, jax.ShapeDtypeStruct((b, tt, d), F32))
            out_specs = (out_specs, pl.BlockSpec((1, tm, d), lambda bb, i, j: (bb, i, 0)))
    return pl.pallas_call(
        kern, out_shape=out_shape, grid=grid, in_specs=in_specs, out_specs=out_specs,
        scratch_shapes=scratch,
        compiler_params=_cparams(("parallel", "parallel", "arbitrary")),
        name=name,
    )(*args)


def _outproj_kernel(*refs, n_lhs, tm, ctx_len):
    lhs = refs[:n_lhs]
    ws = refs[n_lhs:2 * n_lhs]
    x_ref, g_ref, o_ref = refs[2 * n_lhs:]
    i = pl.program_id(1)
    acc = jnp.dot(lhs[0][0], ws[0][...], preferred_element_type=F32)
    for p in range(1, n_lhs):
        acc = acc + jnp.dot(lhs[p][0], ws[p][...], preferred_element_type=F32)
    row = i * tm + lax.broadcasted_iota(I32, (tm, 1), 0)
    g = jnp.where(row < ctx_len, g_ref[0, 0:1, :], g_ref[0, 1:2, :])
    o_ref[0] = x_ref[0] + g * acc


def _outproj(lhs_list, w_list, xa, g_sel, ctx_len):
    b, tt, d = xa.shape
    tm = _pick(tt, (768, 384, 256, 128))
    tn = _pick(d, (512, 256, 128))
    n_lhs = len(lhs_list)
    in_specs = []
    for l in lhs_list:
        in_specs.append(pl.BlockSpec((1, tm, l.shape[2]), lambda bb, i, j: (bb, i, 0)))
    for w in w_list:
        in_specs.append(pl.BlockSpec((w.shape[0], tn), lambda bb, i, j: (0, j)))
    in_specs += [
        pl.BlockSpec((1, tm, tn), lambda bb, i, j: (bb, i, j)),
        pl.BlockSpec((1, 2, tn), lambda bb, i, j: (bb, 0, j)),
    ]
    return pl.pallas_call(
        functools.partial(_outproj_kernel, n_lhs=n_lhs, tm=tm, ctx_len=ctx_len),
        out_shape=jax.ShapeDtypeStruct((b, tt, d), F32),
        grid=(b, tt // tm, d // tn),
        in_specs=in_specs,
        out_specs=pl.BlockSpec((1, tm, tn), lambda bb, i, j: (bb, i, j)),
        compiler_params=_cparams(("parallel", "parallel", "parallel")),
        name="outproj",
    )(*lhs_list, *w_list, xa, g_sel)


def _ret_kernel(*refs, direction, n_ctx_chunks):
    if direction == 0:
        q_ref, k_ref, v_ref, lg_ref, o_ref, st_scr, dm_scr, qd_scr, kd_scr = refs
    else:
        q_ref, k_ref, v_ref, lg_ref, of_ref, g_ref, o_ref, st_scr, dm_scr, qd_scr, kd_scr = refs
    c = pl.program_id(1)
    ii = lax.broadcasted_iota(I32, (CHUNK, CHUNK), 0).astype(F32)
    jj = lax.broadcasted_iota(I32, (CHUNK, CHUNK), 1).astype(F32)

    @pl.when(c == 0)
    def _():
        st_scr[...] = jnp.zeros_like(st_scr)
        for h in range(RET_HEADS):
            lg = -jnp.exp(lg_ref[h])
            if direction == 0:
                diff = ii - jj
                qe = ii + 1.0
                ke = (CHUNK - 1.0) - ii
            else:
                diff = jj - ii
                qe = CHUNK - ii
                ke = ii
            keep = diff >= 0
            dm_scr[h] = jnp.where(keep, jnp.exp(jnp.where(keep, diff, 0.0) * lg), 0.0)
            qd_scr[h] = jnp.exp(qe * lg)
            kd_scr[h] = jnp.exp(ke * lg)

    for h in range(RET_HEADS):
        sl = slice(h * HEAD_DIM, (h + 1) * HEAD_DIM)
        qh = q_ref[0, :, sl]
        kh = k_ref[0, :, sl]
        vh = v_ref[0, :, sl]
        st = st_scr[h]
        s = lax.dot_general(qh, kh, _NT, preferred_element_type=F32) * dm_scr[h]
        inner = jnp.dot(s.astype(BF16), vh, preferred_element_type=F32)
        cross = jnp.dot(qh, st.astype(BF16), preferred_element_type=F32) * qd_scr[h]
        out = inner + cross
        kdec = (kh.astype(F32) * kd_scr[h]).T.astype(BF16)
        cd = jnp.exp(CHUNK * (-jnp.exp(lg_ref[h])))
        st_scr[h] = cd * st + jnp.dot(kdec, vh, preferred_element_type=F32)
        if direction == 0:
            o_ref[0, :, sl] = out
        else:
            o = out + of_ref[0, :, sl]
            mu = jnp.mean(o, axis=-1, keepdims=True)
            var = jnp.mean(jnp.square(o - mu), axis=-1, keepdims=True)
            on = (o - mu) * lax.rsqrt(var + EPS)
            o_ref[0, :, sl] = (on * _silu(g_ref[0, :, sl].astype(F32))).astype(o_ref.dtype)


def _retention(p, lgb, ctx_len):
    b, tt, _ = p.shape
    nc = tt // CHUNK
    ncc = ctx_len // CHUNK
    w = RET_HEADS * HEAD_DIM

    def fwd_chunk(bb, c):
        return c

    def bwd_chunk(bb, c):
        return jnp.where(c < ncc, ncc - 1 - c, nc - 1 + ncc - c)

    def specs(chunk_fn):
        return [
            pl.BlockSpec((1, CHUNK, w), lambda bb, c: (bb, chunk_fn(bb, c), 0)),
            pl.BlockSpec((1, CHUNK, w), lambda bb, c: (bb, chunk_fn(bb, c), 1)),
            pl.BlockSpec((1, CHUNK, w), lambda bb, c: (bb, chunk_fn(bb, c), 2)),
        ]

    scratch = [pltpu.VMEM((RET_HEADS, HEAD_DIM, HEAD_DIM), F32) for _ in range(4)]
    lg_spec = lambda d: pl.BlockSpec((None, RET_HEADS, 1, LANES), lambda bb, c: (d, 0, 0, 0))
    o_f = pl.pallas_call(
        functools.partial(_ret_kernel, direction=0, n_ctx_chunks=ncc),
        out_shape=jax.ShapeDtypeStruct((b, tt, w), F32),
        grid=(b, nc),
        in_specs=specs(fwd_chunk) + [lg_spec(0)],
        out_specs=pl.BlockSpec((1, CHUNK, w), lambda bb, c: (bb, c, 0)),
        scratch_shapes=scratch,
        compiler_params=_cparams(("parallel", "arbitrary")),
        name="retention_fwd",
    )(p, p, p, lgb)
    y = pl.pallas_call(
        functools.partial(_ret_kernel, direction=1, n_ctx_chunks=ncc),
        out_shape=jax.ShapeDtypeStruct((b, tt, w), BF16),
        grid=(b, nc),
        in_specs=specs(bwd_chunk) + [
            lg_spec(1),
            pl.BlockSpec((1, CHUNK, w), lambda bb, c: (bb, bwd_chunk(bb, c), 0)),
            pl.BlockSpec((1, CHUNK, w), lambda bb, c: (bb, bwd_chunk(bb, c), 3)),
        ],
        out_specs=pl.BlockSpec((1, CHUNK, w), lambda bb, c: (bb, bwd_chunk(bb, c), 0)),
        scratch_shapes=scratch,
        compiler_params=_cparams(("parallel", "arbitrary")),
        name="retention_bwd",
    )(p, p, p, lgb, o_f, p)
    return y


def _attn_kernel(q_ref, k_ref, v_ref, o_ref, *, tq, ctx_len):
    i = pl.program_id(2)
    scale_log2e = (HEAD_DIM ** -0.5) * math.log2(math.e)

    def run(masked):
        s = lax.dot_general(q_ref[0], k_ref[0], _NT, preferred_element_type=F32) * scale_log2e
        if masked:
            qrow = i * tq + lax.broadcasted_iota(I32, (tq, 1), 0)
            key = lax.broadcasted_iota(I32, (1, s.shape[1]), 1)
            s = jnp.where((qrow < ctx_len) & (key >= ctx_len), NEG_BIG, s)
        pexp = jnp.exp2(s - jnp.max(s, axis=-1, keepdims=True))
        denom = jnp.sum(pexp, axis=-1, keepdims=True)
        out = jnp.dot(pexp.astype(BF16), v_ref[0], preferred_element_type=F32)
        o_ref[0] = (out / denom).astype(o_ref.dtype)

    has_ctx = i * tq < ctx_len

    @pl.when(has_ctx)
    def _():
        run(True)

    @pl.when(jnp.logical_not(has_ctx))
    def _():
        run(False)


def _attention(p, ctx_len):
    b, tt, _ = p.shape
    tq = _pick(tt, (256, 128))
    groups = ATT_HEADS // ATT_KV_HEADS
    q_base = 4 * RET_HEADS
    k_base = q_base + ATT_HEADS
    v_base = k_base + ATT_KV_HEADS
    return pl.pallas_call(
        functools.partial(_attn_kernel, tq=tq, ctx_len=ctx_len),
        out_shape=jax.ShapeDtypeStruct((b, tt, ATT_HEADS * HEAD_DIM), BF16),
        grid=(b, ATT_KV_HEADS, tt // tq, groups),
        in_specs=[
            pl.BlockSpec((1, tq, HEAD_DIM), lambda bb, kv, i, g: (bb, i, q_base + kv * groups + g)),
            pl.BlockSpec((1, tt, HEAD_DIM), lambda bb, kv, i, g: (bb, 0, k_base + kv)),
            pl.BlockSpec((1, tt, HEAD_DIM), lambda bb, kv, i, g: (bb, 0, v_base + kv)),
        ],
        out_specs=pl.BlockSpec((1, tq, HEAD_DIM), lambda bb, kv, i, g: (bb, i, kv * groups + g)),
        compiler_params=_cparams(("parallel", "parallel", "parallel", "arbitrary")),
        name="gqa_attention",
    )(p, p, p)


def _conv_kernel(main_ref, prev_ref, next_ref, w_ref, b_ref, o_ref, ext_scr, *, tr, halo, ctx_len, tt):
    i = pl.program_id(1)
    start = i * tr
    seg_lo = jnp.where(start < ctx_len, 0, ctx_len)
    seg_hi = jnp.where(start < ctx_len, ctx_len, tt)
    ext = jnp.concatenate([prev_ref[0], main_ref[0], next_ref[0]], axis=0).astype(F32)
    row = start - halo + lax.broadcasted_iota(I32, (tr + 2 * halo, 1), 0)
    ext_scr[...] = jnp.where((row >= seg_lo) & (row < seg_hi), ext, 0.0)
    pad = D_CONV // 2
    acc = b_ref[...] + w_ref[0:1, :] * ext_scr[pl.ds(halo - pad, tr), :]
    for k in range(1, D_CONV):
        acc = acc + w_ref[k:k + 1, :] * ext_scr[pl.ds(halo - pad + k, tr), :]
    o_ref[0] = _silu(acc).astype(o_ref.dtype)


def _conv(p, col0, width, conv_w, conv_b, ctx_len):
    b, tt, _ = p.shape
    tr = CHUNK
    halo = 16
    tc = _pick(math.gcd(width, col0), (2048, 1024, 512, 256, 128))
    assert ctx_len % tr == 0 and col0 % tc == 0
    cb0 = col0 // tc
    nblk = tt // halo
    r = tr // halo
    return pl.pallas_call(
        functools.partial(_conv_kernel, tr=tr, halo=halo, ctx_len=ctx_len, tt=tt),
        out_shape=jax.ShapeDtypeStruct((b, tt, width), BF16),
        grid=(b, tt // tr, width // tc),
        in_specs=[
            pl.BlockSpec((1, tr, tc), lambda bb, i, j: (bb, i, cb0 + j)),
            pl.BlockSpec((1, halo, tc), lambda bb, i, j: (bb, jnp.maximum(i * r - 1, 0), cb0 + j)),
            pl.BlockSpec((1, halo, tc), lambda bb, i, j: (bb, jnp.minimum((i + 1) * r, nblk - 1), cb0 + j)),
            pl.BlockSpec((D_CONV, tc), lambda bb, i, j: (0, j)),
            pl.BlockSpec((1, tc), lambda bb, i, j: (0, j)),
        ],
        out_specs=pl.BlockSpec((1, tr, tc), lambda bb, i, j: (bb, i, j)),
        scratch_shapes=[pltpu.VMEM((tr + 2 * halo, tc), F32)],
        compiler_params=_cparams(("parallel", "parallel", "parallel")),
        name="ssd_conv",
    )(p, p, p, conv_w, conv_b)


def _ssd_kernel(*refs, direction):
    if direction == 0:
        x_ref, b_ref, c_ref, dt_ref, bias_ref, alog_ref, o_ref, st_scr = refs
    else:
        (x_ref, b_ref, c_ref, dt_ref, bias_ref, alog_ref, yf_ref, z_ref, dsk_ref, nw_ref,
         o_ref, st_scr) = refs
    cstep = pl.program_id(1)

    @pl.when(cstep == 0)
    def _():
        st_scr[...] = jnp.zeros_like(st_scr)

    n_heads_dir = LANES // 2
    dt = _softplus(dt_ref[0] + bias_ref[...])
    dt_t = dt.T
    a_col = -jnp.exp(alog_ref[...])
    dta_t = dt_t * a_col
    kk = lax.broadcasted_iota(I32, (CHUNK, CHUNK), 0)
    jj = lax.broadcasted_iota(I32, (CHUNK, CHUNK), 1)
    if direction == 0:
        tri = (kk <= jj).astype(F32)
        keep = kk >= jj
        end_row = CHUNK - 1
    else:
        tri = (kk >= jj).astype(F32)
        keep = kk <= jj
        end_row = 0
    cs_t = jnp.dot(dta_t, tri, preferred_element_type=F32, precision=lax.Precision.HIGHEST)
    cs = cs_t.T
    lane = lax.broadcasted_iota(I32, (1, LANES), 1)
    lo = lane < SSM_HEAD_DIM
    hpg = n_heads_dir // N_GROUPS

    for g in range(N_GROUPS):
        gs = slice(g * D_STATE, (g + 1) * D_STATE)
        cmat = c_ref[0, :, gs]
        bmat = b_ref[0, :, gs]
        b_t = bmat.astype(F32).T.astype(BF16)
        cb = jnp.dot(cmat, b_t, preferred_element_type=F32)
        ysq = jnp.zeros((CHUNK, 1), F32)
        ypairs = []
        for pr in range(hpg // 2):
            h0 = g * hpg + 2 * pr
            hd0 = direction * n_heads_dir + h0
            xs = slice(h0 * SSM_HEAD_DIM, (h0 + 2) * SSM_HEAD_DIM)
            xpair = x_ref[0, :, xs]
            cs_b = []
            dt_b = []
            ydiag = []
            for u in range(2):
                hd = hd0 + u
                cs_col = jnp.broadcast_to(cs[:, hd:hd + 1], (CHUNK, LANES))
                dt_col = jnp.broadcast_to(dt[:, hd:hd + 1], (CHUNK, LANES))
                seg = cs_col - cs_t[hd:hd + 1, :]
                lm = jnp.where(keep, jnp.exp(jnp.where(keep, seg, 0.0)), 0.0)
                wmat = cb * lm * dt_t[hd:hd + 1, :]
                ydiag.append(jnp.dot(wmat.astype(BF16), xpair, preferred_element_type=F32))
                cs_b.append(cs_col)
                dt_b.append(dt_col)
            cs_pair = jnp.where(lo, cs_b[0], cs_b[1])
            dt_pair = jnp.where(lo, dt_b[0], dt_b[1])
            e_pair = jnp.exp(cs_pair)
            tot = cs_pair[end_row:end_row + 1, :]
            st = st_scr[g, pr]
            y_off = jnp.dot(cmat, st.astype(BF16), preferred_element_type=F32) * e_pair
            y = jnp.where(lo, ydiag[0], ydiag[1]) + y_off
            to_end = jnp.exp(tot - cs_pair) * dt_pair
            xw = (xpair.astype(F32) * to_end).astype(BF16)
            st_scr[g, pr] = jnp.exp(tot) * st + jnp.dot(b_t, xw, preferred_element_type=F32)
            if direction == 0:
                o_ref[0, :, xs] = y
            else:
                y = y + yf_ref[0, :, xs] + dsk_ref[:, xs] * xpair.astype(F32)
                y = y * _silu(z_ref[0, :, xs].astype(F32))
                ysq = ysq + jnp.sum(y * y, axis=-1, keepdims=True)
                ypairs.append((xs, y))
        if direction == 1:
            inv = lax.rsqrt(ysq / (hpg * SSM_HEAD_DIM) + EPS)
            for xs, y in ypairs:
                o_ref[0, :, xs] = (y * inv * nw_ref[:, xs]).astype(o_ref.dtype)


def _ssd(xbc, dt_raw, p, dt_bias_row, alog_b, dskip_row, norm_w_row, ctx_len):
    b, tt, _ = xbc.shape
    nc = tt // CHUNK
    ncc = ctx_len // CHUNK
    d_inner = dskip_row.shape[1]
    gw = N_GROUPS * D_STATE
    nb = d_inner // gw

    def fwd_chunk(c):
        return c

    def bwd_chunk(c):
        return jnp.where(c < ncc, ncc - 1 - c, nc - 1 + ncc - c)

    def specs(cf):
        return [
            pl.BlockSpec((1, CHUNK, d_inner), lambda bb, c: (bb, cf(c), 0)),
            pl.BlockSpec((1, CHUNK, gw), lambda bb, c: (bb, cf(c), nb)),
            pl.BlockSpec((1, CHUNK, gw), lambda bb, c: (bb, cf(c), nb + 1)),
            pl.BlockSpec((1, CHUNK, LANES), lambda bb, c: (bb, cf(c), 0)),
            pl.BlockSpec((1, LANES), lambda bb, c: (0, 0)),
            pl.BlockSpec((LANES, LANES), lambda bb, c: (0, 0)),
        ]

    hpg = (d_inner // SSM_HEAD_DIM) // N_GROUPS
    scratch = [pltpu.VMEM((N_GROUPS, hpg // 2, D_STATE, 2 * SSM_HEAD_DIM), F32)]
    y_f = pl.pallas_call(
        functools.partial(_ssd_kernel, direction=0),
        out_shape=jax.ShapeDtypeStruct((b, tt, d_inner), F32),
        grid=(b, nc),
        in_specs=specs(fwd_chunk),
        out_specs=pl.BlockSpec((1, CHUNK, d_inner), lambda bb, c: (bb, c, 0)),
        scratch_shapes=scratch,
        compiler_params=_cparams(("parallel", "arbitrary")),
        name="ssd_fwd",
    )(xbc, xbc, xbc, dt_raw, dt_bias_row, alog_b)
    yn = pl.pallas_call(
        functools.partial(_ssd_kernel, direction=1),
        out_shape=jax.ShapeDtypeStruct((b, tt, d_inner), BF16),
        grid=(b, nc),
        in_specs=specs(bwd_chunk) + [
            pl.BlockSpec((1, CHUNK, d_inner), lambda bb, c: (bb, bwd_chunk(c), 0)),
            pl.BlockSpec((1, CHUNK, d_inner), lambda bb, c: (bb, bwd_chunk(c), 0)),
            pl.BlockSpec((1, d_inner), lambda bb, c: (0, 0)),
            pl.BlockSpec((1, d_inner), lambda bb, c: (0, 0)),
        ],
        out_specs=pl.BlockSpec((1, CHUNK, d_inner), lambda bb, c: (bb, bwd_chunk(c), 0)),
        scratch_shapes=scratch,
        compiler_params=_cparams(("parallel", "arbitrary")),
        name="ssd_bwd",
    )(xbc, xbc, xbc, dt_raw, dt_bias_row, alog_b, y_f, p, dskip_row, norm_w_row)
    return yn


def _topk_rows(sc, n_rows, k):
    rid = lax.broadcasted_iota(I32, sc.shape, 0)
    vals = []
    idxs = []
    for _ in range(k):
        m = jnp.max(sc, axis=0, keepdims=True)
        idx = jnp.min(jnp.where(sc == m, rid, n_rows), axis=0, keepdims=True)
        vals.append(m)
        idxs.append(idx)
        sc = jnp.where(rid == idx, -jnp.inf, sc)
    return jnp.concatenate(vals, axis=0), jnp.concatenate(idxs, axis=0), rid


def _route_kernel(q_ref, keys_ref, ex_ref, gate_ref, ext_scr, gt_scr):
    def head_body(h, carry):
        side = []
        for s in range(2):
            col = pl.multiple_of((h * 2 + s) * HEAD_DIM, HEAD_DIM)
            qs = q_ref[:, pl.ds(col, HEAD_DIM)].astype(BF16)
            ky = keys_ref[h, s]
            sc = lax.dot_general(ky, qs, _NT, preferred_element_type=F32)
            v, ix, _ = _topk_rows(sc, N_KEYS, PEER_TOPK)
            side.append((v, ix))
        (s1, i1), (s2, i2) = side
        r8 = lax.broadcasted_iota(I32, (8, 1), 0)
        blocks = []
        for half in range(2):
            rows = slice(8 * half, 8 * half + 8)
            blocks.append((s1[0:1, :] + s2[rows, :], i1[0:1, :] * N_KEYS + i2[rows, :], 8 * half + r8))
        for a in range(1, 8):
            nb = PEER_TOPK // (a + 1)
            val = s1[a:a + 1, :] + s2[0:8, :]
            if nb < 8:
                val = jnp.where(r8 < nb, val, -jnp.inf)
            blocks.append((val, i1[a:a + 1, :] * N_KEYS + i2[0:8, :], PEER_TOPK * a + r8))
        blocks.append((s1[8:16, :] + s2[0:1, :], i1[8:16, :] * N_KEYS + i2[0:1, :],
                       PEER_TOPK * (8 + r8)))
        cand = jnp.concatenate([blk[0] for blk in blocks], axis=0)
        cidx = jnp.concatenate([blk[1] for blk in blocks], axis=0)
        flat = jnp.concatenate([blk[2] for blk in blocks], axis=0)
        best = []
        exps = []
        for _ in range(PEER_TOPK):
            m = jnp.max(cand, axis=0, keepdims=True)
            pos = jnp.min(jnp.where(cand == m, flat, PEER_TOPK * PEER_TOPK), axis=0, keepdims=True)
            hit = flat == pos
            exps.append(jnp.max(jnp.where(hit, cidx, -1), axis=0, keepdims=True))
            best.append(m)
            cand = jnp.where(hit, -jnp.inf, cand)
        best = jnp.concatenate(best, axis=0)
        e = jnp.exp(best - best[0:1, :])
        gate = e / jnp.sum(e, axis=0, keepdims=True)
        row0 = pl.multiple_of(h * PEER_TOPK, PEER_TOPK)
        ext_scr[pl.ds(row0, PEER_TOPK), :] = jnp.concatenate(exps, axis=0)
        gt_scr[pl.ds(row0, PEER_TOPK), :] = gate
        return carry

    lax.fori_loop(0, PEER_HEADS, head_body, 0, unroll=4)
    ex_ref[...] = ext_scr[...].T
    gate_ref[...] = gt_scr[...].T


def _route(q, keys_bf16):
    n, dq = q.shape
    tl = LANES
    npair = PEER_HEADS * PEER_TOPK
    return pl.pallas_call(
        _route_kernel,
        out_shape=(jax.ShapeDtypeStruct((n, npair), I32), jax.ShapeDtypeStruct((n, npair), F32)),
        grid=(n // tl,),
        in_specs=[
            pl.BlockSpec((tl, dq), lambda i: (i, 0)),
            pl.BlockSpec(keys_bf16.shape, lambda i: (0, 0, 0, 0)),
        ],
        out_specs=(pl.BlockSpec((tl, npair), lambda i: (i, 0)), pl.BlockSpec((tl, npair), lambda i: (i, 0))),
        scratch_shapes=[pltpu.VMEM((npair, tl), I32), pltpu.VMEM((npair, tl), F32)],
        compiler_params=_cparams(("parallel",)),
        name="peer_route",
    )(q, keys_bf16)


ROWS_PER_EXPERT = 16
N_SLOTS = 10
LOOKAHEAD = 8
TOKENS_PER_STEP = 2
GROUP = 8


def _peer_kernel(ex_ref, h_ref, gate_ref, x_ref, gmod_ref, tab_ref, o_ref, buf, sems, w_scr, *, tb, npair):
    rpe = ROWS_PER_EXPERT
    rows = npair * rpe
    n_groups = npair // GROUP
    sub = lax.broadcasted_iota(I32, (8, LANES), 0)
    lane = lax.broadcasted_iota(I32, (8, LANES), 1)
    masks = [(sub & sh) == 0 for sh in (1, 2, 4)]
    hi_mask = jnp.uint32(0xFFFF0000)

    def wait_slot(slot):
        pltpu.make_async_copy(tab_ref.at[pl.ds(0, rows), :], buf.at[slot], sems.at[slot]).wait()

    def issue(t, slot, j0, j1):
        for j in range(j0, j1):
            e = ex_ref[t, j]
            src = tab_ref.at[pl.ds(pl.multiple_of(e * rpe, rpe), rpe), :]
            pltpu.make_async_copy(src, buf.at[slot, pl.ds(j * rpe, rpe), :], sems.at[slot]).start(priority=j % 2)

    def combine(a, b, level):
        m = masks[level]
        return jnp.where(m, a, b) + pltpu.roll(jnp.where(m, b, a), 1 << level, 0)

    def row_to_tile(row, s0):
        tile = jnp.broadcast_to(row[:, s0 * LANES:(s0 + 1) * LANES], (8, LANES))
        for s in range(1, 8):
            piece = jnp.broadcast_to(row[:, (s0 + s) * LANES:(s0 + s + 1) * LANES], (8, LANES))
            tile = jnp.where(sub == s, piece, tile)
        return tile

    def load_token(t):
        hrow = h_ref[pl.ds(t, 1), :]
        h0 = row_to_tile(hrow, 0)
        h1 = row_to_tile(hrow, 8)
        grow = jnp.broadcast_to(gate_ref[pl.ds(t, 1), :], (8, LANES))
        return t, h0, h1, grow

    def group_step(t, tok, g, acc, u):
        _, h0, h1, grow = tok
        slot = t % N_SLOTS
        acc0, acc1 = acc
        parts = []
        words = []
        for k in range(GROUP):
            base = (g * GROUP + k) * rpe
            x0 = buf[slot, pl.ds(base, 8), :]
            x1 = buf[slot, pl.ds(base + 8, 8), :]
            words.append((x0, x1))
            parts.append(pltpu.bitcast(x0 << 16, F32) * h0 + pltpu.bitcast(x1 << 16, F32) * h1)
        for level in range(3):
            parts = [combine(parts[2 * i], parts[2 * i + 1], level) for i in range(len(parts) // 2)]
        act = jnp.sum(parts[0], axis=-1, keepdims=True)
        gcol = jnp.sum(jnp.where(lane == g * GROUP + sub, grow, 0.0), axis=-1, keepdims=True)
        wcol = 0.5 * act * (1.0 + lax.erf(act * (2.0 ** -0.5))) * gcol
        w_scr[u, pl.ds(g * GROUP, GROUP), :] = jnp.broadcast_to(wcol, (GROUP, LANES))
        for k in range(GROUP):
            wb = jnp.broadcast_to(w_scr[u, pl.ds(g * GROUP + k, 1), :], (8, LANES))
            acc0 = acc0 + wb * pltpu.bitcast(words[k][0] & hi_mask, F32)
            acc1 = acc1 + wb * pltpu.bitcast(words[k][1] & hi_mask, F32)
        return acc0, acc1

    def store_token(tok, acc):
        t = tok[0]
        out_row = jnp.concatenate([a[s:s + 1, :] for a in acc for s in range(8)], axis=1)
        o_ref[pl.ds(t, 1), :] = x_ref[pl.ds(t, 1), :] + gmod_ref[0] * out_row

    def tokens(t_first, refill):
        ts = [t_first + u for u in range(TOKENS_PER_STEP)]
        for t in ts:
            wait_slot(t % N_SLOTS)
        toks = [load_token(t) for t in ts]
        accs = [(jnp.zeros((8, LANES), F32), jnp.zeros((8, LANES), F32)) for _ in ts]
        for g in range(n_groups):
            if refill:
                for t in ts:
                    issue(t + LOOKAHEAD, (t + LOOKAHEAD) % N_SLOTS, g * GROUP, (g + 1) * GROUP)
            accs = [group_step(t, tok, g, acc, u) for u, (t, tok, acc) in enumerate(zip(ts, toks, accs))]
        for tok, acc in zip(toks, accs):
            store_token(tok, acc)

    for t0 in range(LOOKAHEAD):
        issue(t0, t0, 0, npair)

    def main_body(it, carry):
        tokens(it * TOKENS_PER_STEP, True)
        return carry

    def tail_body(it, carry):
        tokens(it * TOKENS_PER_STEP, False)
        return carry

    n_main = (tb - LOOKAHEAD) // TOKENS_PER_STEP
    lax.fori_loop(0, n_main, main_body, 0)
    lax.fori_loop(n_main, tb // TOKENS_PER_STEP, tail_body, 0)


def _peer(ex, h2, gate, x2, g_sel, table, tt, ctx_len):
    n, npair = ex.shape
    rpe = ROWS_PER_EXPERT
    d = rpe * LANES
    tb = _pick(math.gcd(tt, ctx_len), (128, 64, 32, 16, 8))
    assert tb >= N_SLOTS and h2.shape == (n, d) and npair % GROUP == 0
    assert (tb - LOOKAHEAD) % TOKENS_PER_STEP == 0 and tb % TOKENS_PER_STEP == 0
    assert N_SLOTS - LOOKAHEAD >= TOKENS_PER_STEP
    g_rows = g_sel.reshape(-1, 1, d)

    def g_index(i):
        r = i * tb
        return ((r // tt) * 2 + jnp.where(r % tt < ctx_len, 0, 1), 0, 0)

    return pl.pallas_call(
        functools.partial(_peer_kernel, tb=tb, npair=npair),
        out_shape=jax.ShapeDtypeStruct((n, d), F32),
        grid=(n // tb,),
        in_specs=[
            pl.BlockSpec((tb, npair), lambda i: (i, 0), memory_space=pltpu.SMEM),
            pl.BlockSpec((tb, d), lambda i: (i, 0)),
            pl.BlockSpec((tb, npair), lambda i: (i, 0)),
            pl.BlockSpec((tb, d), lambda i: (i, 0)),
            pl.BlockSpec((1, 1, d), g_index),
            pl.BlockSpec(memory_space=pl.ANY),
        ],
        out_specs=pl.BlockSpec((tb, d), lambda i: (i, 0)),
        scratch_shapes=[pltpu.VMEM((N_SLOTS, npair * rpe, LANES), jnp.uint32),
                        pltpu.SemaphoreType.DMA((N_SLOTS,)),
                        pltpu.VMEM((TOKENS_PER_STEP, npair, LANES), F32)],
        compiler_params=_cparams(("arbitrary",)),
        name="peer_experts",
    )(ex, h2, gate, x2, g_rows, table)


def _final_kernel(x_ref, w_ref, o_ref):
    x = x_ref[0]
    o_ref[0] = x * lax.rsqrt(jnp.mean(x * x, axis=-1, keepdims=True) + EPS) * w_ref[...]


def _final_norm(xa, w_row, ctx_len):
    b, tt, d = xa.shape
    t = tt - ctx_len
    tr = _pick(math.gcd(t, ctx_len), (256, 128))
    off = ctx_len // tr
    return pl.pallas_call(
        _final_kernel,
        out_shape=jax.ShapeDtypeStruct((b, t, d), F32),
        grid=(b, t // tr),
        in_specs=[pl.BlockSpec((1, tr, d), lambda bb, i: (bb, i + off, 0)),
                  pl.BlockSpec((1, d), lambda bb, i: (0, 0))],
        out_specs=pl.BlockSpec((1, tr, d), lambda bb, i: (bb, i, 0)),
        compiler_params=_cparams(("parallel", "parallel")),
        name="final_norm",
    )(xa, w_row)


def _rope_tables(t, ctx_len):
    tok = jnp.arange(t, dtype=I32)
    row = (tok // GRID_W).astype(F32)
    col = (tok % GRID_W).astype(F32)
    axis_dim = HEAD_DIM // 2
    inv = ROPE_THETA ** (-jnp.arange(0, axis_dim, 2, dtype=F32) / axis_dim)
    ang_r = row[:, None] * inv[None, :]
    ang_c = col[:, None] * inv[None, :]
    cos = jnp.concatenate([jnp.cos(ang_r), jnp.cos(ang_r), jnp.cos(ang_c), jnp.cos(ang_c)], axis=-1)
    sin = jnp.concatenate([-jnp.sin(ang_r), jnp.sin(ang_r), -jnp.sin(ang_c), jnp.sin(ang_c)], axis=-1)
    cos = jnp.concatenate([jnp.ones((ctx_len, HEAD_DIM), F32), cos], axis=0)
    sin = jnp.concatenate([jnp.zeros((ctx_len, HEAD_DIM), F32), sin], axis=0)
    return cos, sin


def _pack_kernel(u_ref, v_ref, o_ref, *, te):
    rpe = ROWS_PER_EXPERT
    for s in range(rpe):
        ls = slice(s * LANES, (s + 1) * LANES)
        ub = pltpu.bitcast(u_ref[:, ls].astype(BF16).astype(F32), jnp.uint32)
        vb = pltpu.bitcast(v_ref[:, ls].astype(BF16).astype(F32), jnp.uint32)
        o_ref[pl.ds(s, te, stride=rpe), :] = (ub >> 16) | vb


def _pack_experts(u, v):
    e, d = u.shape
    rpe = d // LANES
    assert rpe == ROWS_PER_EXPERT
    te = _pick(e, (256, 128, 64, 32, 16, 8))
    return pl.pallas_call(
        functools.partial(_pack_kernel, te=te),
        out_shape=jax.ShapeDtypeStruct((e * rpe, LANES), jnp.uint32),
        grid=(e // te,),
        in_specs=[pl.BlockSpec((te, d), lambda i: (i, 0)), pl.BlockSpec((te, d), lambda i: (i, 0))],
        out_specs=pl.BlockSpec((te * rpe, LANES), lambda i: (i, 0)),
        compiler_params=_cparams(("parallel",)),
        name="pack_experts",
    )(u, v)


def _mod_sel(mods_l, chunk, b, d):
    sl = slice(chunk * d, (chunk + 1) * d)
    ctx_row = jnp.broadcast_to(mods_l[b:b + 1, sl], (b, d))
    return jnp.stack([ctx_row, mods_l[:b, sl]], axis=1)


def kernel(x, c, ctx, c_ctx, ada_w, ada_b, norm1_w, norm2_w, ev_w_in, ev_w_out, ev_ret_decay, ev_q_norm, ev_k_norm, od_w_in, od_conv_w, od_conv_b, od_dt_bias, od_a_log, od_d, od_norm_w, od_w_out, peer_wq, peer_keys, peer_u, peer_v, final_norm_w):
    b, t, d = x.shape
    ctx_len = ctx.shape[1]
    depth = ada_w.shape[0]
    tt = ctx_len + t
    assert b + 1 <= 8 and ctx_len % CHUNK == 0 and t % CHUNK == 0 and t % GRID_W == 0
    assert d == ROWS_PER_EXPERT * LANES

    xa = jnp.concatenate([ctx, x], axis=1)
    crow = jnp.zeros((8, d), F32).at[:b].set(c).at[b].set(c_ctx)
    mods = _mods(crow, ada_w, ada_b)
    cos, sin = _rope_tables(t, ctx_len)

    ret_w = RET_HEADS * HEAD_DIM
    d_inner = od_w_out.shape[1]
    n_ssm_heads = d_inner // SSM_HEAD_DIM
    zx_cols = 2 * d_inner + 2 * N_GROUPS * D_STATE

    for layer in range(depth):
        m = mods[layer]
        sel = [_mod_sel(m, k, b, d) for k in range(N_MOD)]
        j = layer // 2
        nw1 = norm1_w[layer].reshape(1, d)
        if layer % 2 == 0:
            p = _proj(xa, nw1, sel[0], sel[1], ev_w_in[j].astype(BF16), BF16, ctx_len,
                      even_extras=(cos, sin, ev_q_norm[j].reshape(1, HEAD_DIM), ev_k_norm[j].reshape(1, HEAD_DIM)))
            lgb = jnp.broadcast_to(ev_ret_decay[j].astype(F32)[:, :, None, None], (2, RET_HEADS, 1, LANES))
            y_ret = _retention(p, lgb, ctx_len)
            y_att = _attention(p, ctx_len)
            w_out = ev_w_out[j].astype(BF16)
            xa = _outproj([y_ret, y_att], [w_out[:ret_w], w_out[ret_w:]], xa, sel[2], ctx_len)
        else:
            w_in = od_w_in[j]
            p = _proj(xa, nw1, sel[0], sel[1], w_in[:, :zx_cols].astype(BF16), BF16, ctx_len)
            dt_raw = _proj(xa, nw1, sel[0], sel[1], w_in[:, zx_cols:].astype(BF16), F32, ctx_len)
            xbc = _conv(p, d_inner, zx_cols - d_inner, od_conv_w[j], od_conv_b[j].reshape(1, -1), ctx_len)
            alog_b = jnp.broadcast_to(od_a_log[j].astype(F32).reshape(2 * n_ssm_heads, 1), (2 * n_ssm_heads, LANES))
            dskip_row = jnp.repeat(od_d[j].astype(F32), SSM_HEAD_DIM).reshape(1, d_inner)
            yn = _ssd(xbc, dt_raw, p, od_dt_bias[j].reshape(1, 2 * n_ssm_heads), alog_b, dskip_row,
                      od_norm_w[j].reshape(1, d_inner), ctx_len)
            xa = _outproj([yn], [od_w_out[j].astype(BF16)], xa, sel[2], ctx_len)

        q, h2 = _proj(xa, norm2_w[layer].reshape(1, d), sel[3], sel[4], peer_wq[layer].astype(BF16), F32,
                      ctx_len, emit_h=True)
        n = b * tt
        ex, gate = _route(q.reshape(n, -1), peer_keys[layer].astype(BF16))
        table = _pack_experts(peer_u[layer], peer_v[layer])
        xa = _peer(ex, h2.reshape(n, d), gate, xa.reshape(n, d), sel[5], table, tt, ctx_len).reshape(b, tt, d)

    return _final_norm(xa, final_norm_w.reshape(1, d), ctx_len)
```

```python
import functools
import math

import jax
import jax.numpy as jnp
from jax import lax
from jax.experimental import pallas as pl
from jax.experimental.pallas import tpu as pltpu

F32 = jnp.float32
BF16 = jnp.bfloat16
I32 = jnp.int32

EPS = 1e-6
HEAD_DIM = 128
RET_HEADS = 8
ATT_HEADS = 8
ATT_KV_HEADS = 2
CHUNK = 128
GRID_W = 64
ROPE_THETA = 10000.0
SSM_HEAD_DIM = 64
D_STATE = 128
N_GROUPS = 8
D_CONV = 5
PEER_HEADS = 8
N_KEYS = 128
PEER_TOPK = 16
N_MOD = 6

LANES = 128
V7X_VMEM_BYTES = 64 * 1024 * 1024
VMEM_LIMIT = 52 * 1024 * 1024
NEG_BIG = -1e30

_NT = (((1,), (1,)), ((), ()))


def _pick(n, candidates):
    for c in candidates:
        if n % c == 0:
            return c
    raise ValueError(f"no tile for {n} in {candidates}")


def _cparams(sem, vmem=VMEM_LIMIT):
    return pltpu.CompilerParams(dimension_semantics=sem, vmem_limit_bytes=vmem)


def _sigmoid(x):
    return 1.0 / (1.0 + jnp.exp(-x))


def _silu(x):
    return x * _sigmoid(x)


def _softplus(x):
    return jnp.maximum(x, 0.0) + jnp.log(1.0 + jnp.exp(-jnp.abs(x)))


def _mods_kernel(c_ref, w_ref, b_ref, o_ref):
    sc = _silu(c_ref[...])
    acc = jnp.dot(sc.astype(BF16), w_ref[0].astype(BF16), preferred_element_type=F32)
    o_ref[0] = acc + b_ref[0]


def _mods(crow, ada_w, ada_b):
    depth, d, n = ada_w.shape
    tn = _pick(n, (1024, 512, 256, 128))
    return pl.pallas_call(
        _mods_kernel,
        out_shape=jax.ShapeDtypeStruct((depth, 8, n), F32),
        grid=(depth, n // tn),
        in_specs=[
            pl.BlockSpec((8, d), lambda l, j: (0, 0)),
            pl.BlockSpec((1, d, tn), lambda l, j: (l, 0, j)),
            pl.BlockSpec((1, 1, tn), lambda l, j: (l, 0, j)),
        ],
        out_specs=pl.BlockSpec((1, 8, tn), lambda l, j: (l, 0, j)),
        compiler_params=_cparams(("parallel", "parallel")),
        name="adaln_mods",
    )(crow, ada_w, ada_b.reshape(depth, 1, n))


def _norm_mod(x_ref, nw_ref, sh_ref, sc_ref, i, tm, ctx_len):
    x = x_ref[0]
    ms = jnp.mean(x * x, axis=-1, keepdims=True)
    y = x * lax.rsqrt(ms + EPS) * nw_ref[...]
    row = i * tm + lax.broadcasted_iota(I32, (tm, 1), 0)
    is_ctx = row < ctx_len
    sh = jnp.where(is_ctx, sh_ref[0, 0:1, :], sh_ref[0, 1:2, :])
    sc = jnp.where(is_ctx, sc_ref[0, 0:1, :], sc_ref[0, 1:2, :])
    return y * (1.0 + sc) + sh


def _rope(v, cos, sin_signed, lane_lo):
    rot = jnp.where(lane_lo, pltpu.roll(v, 96, 1), pltpu.roll(v, 32, 1))
    return v * cos + rot * sin_signed


def _head_rms(v, w):
    return v * lax.rsqrt(jnp.mean(v * v, axis=-1, keepdims=True) + EPS) * w


def _proj_plain_kernel(x_ref, nw_ref, sh_ref, sc_ref, w_ref, o_ref, *rest, tm, ctx_len, emit_h):
    if emit_h:
        h_out, h_scr = rest
    else:
        (h_scr,) = rest
    i = pl.program_id(1)
    j = pl.program_id(2)

    @pl.when(j == 0)
    def _():
        h = _norm_mod(x_ref, nw_ref, sh_ref, sc_ref, i, tm, ctx_len)
        h_scr[...] = h.astype(BF16)
        if emit_h:
            h_out[0] = h

    acc = jnp.dot(h_scr[...], w_ref[...], preferred_element_type=F32)
    o_ref[0] = acc.astype(o_ref.dtype)


def _proj_even_kernel(x_ref, nw_ref, sh_ref, sc_ref, w_ref, cos_ref, sin_ref, qn_ref, kn_ref,
                      o_ref, h_scr, *, tm, ctx_len, tn):
    i = pl.program_id(1)
    j = pl.program_id(2)

    @pl.when(j == 0)
    def _():
        h = _norm_mod(x_ref, nw_ref, sh_ref, sc_ref, i, tm, ctx_len)
        h_scr[...] = h.astype(BF16)

    acc = jnp.dot(h_scr[...], w_ref[...], preferred_element_type=F32)
    nh = tn // HEAD_DIM
    cos = cos_ref[...]
    sin = sin_ref[...]
    lane = lax.broadcasted_iota(I32, (1, HEAD_DIM), 1)
    lane_lo = (lane % 64) < 32
    k_scale = HEAD_DIM ** -0.5

    def heads(fn):
        for k in range(nh):
            sl = slice(k * HEAD_DIM, (k + 1) * HEAD_DIM)
            o_ref[0, :, sl] = fn(acc[:, sl], k).astype(o_ref.dtype)

    @pl.when(j < 2)
    def _():
        heads(lambda v, k: _rope(v, cos, sin, lane_lo))

    @pl.when((j >= 2) & (j < 4))
    def _():
        heads(lambda v, k: _rope(v, cos, sin, lane_lo) * k_scale)

    @pl.when((j >= 4) & (j < 8))
    def _():
        o_ref[0] = acc.astype(o_ref.dtype)

    @pl.when((j >= 8) & (j < 10))
    def _():
        heads(lambda v, k: _rope(_head_rms(v, qn_ref[...]), cos, sin, lane_lo))

    @pl.when(j == 10)
    def _():
        heads(lambda v, k: _rope(_head_rms(v, kn_ref[...]), cos, sin, lane_lo) if k < ATT_KV_HEADS else v)


def _proj(xa, nw, sh_sel, sc_sel, w, out_dtype, ctx_len, *, emit_h=False, even_extras=None):
    b, tt, d = xa.shape
    n = w.shape[1]
    tm = _pick(tt, (768, 384, 256, 128))
    tn = _pick(n, (512, 256, 128))
    grid = (b, tt // tm, n // tn)
    in_specs = [
        pl.BlockSpec((1, tm, d), lambda bb, i, j: (bb, i, 0)),
        pl.BlockSpec((1, d), lambda bb, i, j: (0, 0)),
        pl.BlockSpec((1, 2, d), lambda bb, i, j: (bb, 0, 0)),
        pl.BlockSpec((1, 2, d), lambda bb, i, j: (bb, 0, 0)),
        pl.BlockSpec((d, tn), lambda bb, i, j: (0, j)),
    ]
    args = [xa, nw, sh_sel, sc_sel, w]
    out_shape = jax.ShapeDtypeStruct((b, tt, n), out_dtype)
    out_specs = pl.BlockSpec((1, tm, tn), lambda bb, i, j: (bb, i, j))
    scratch = [pltpu.VMEM((tm, d), BF16)]
    if even_extras is not None:
        assert tn == 512 and n == 11 * 512
        cos, sin, qn, kn = even_extras
        in_specs += [
            pl.BlockSpec((tm, HEAD_DIM), lambda bb, i, j: (i, 0)),
            pl.BlockSpec((tm, HEAD_DIM), lambda bb, i, j: (i, 0)),
            pl.BlockSpec((1, HEAD_DIM), lambda bb, i, j: (0, 0)),
            pl.BlockSpec((1, HEAD_DIM), lambda bb, i, j: (0, 0)),
        ]
        args += [cos, sin, qn, kn]
        kern = functools.partial(_proj_even_kernel, tm=tm, ctx_len=ctx_len, tn=tn)
        name = "proj_even"
    else:
        kern = functools.partial(_proj_plain_kernel, tm=tm, ctx_len=ctx_len, emit_h=emit_h)
        name = "proj_plain"
        if emit_h:
            out_shape = (out_shape, jax.ShapeDtypeStruct((b, tt, d), F32))
            out_specs = (out_specs, pl.BlockSpec((1, tm, d), lambda bb, i, j: (bb, i, 0)))
    return pl.pallas_call(
        kern, out_shape=out_shape, grid=grid, in_specs=in_specs, out_specs=out_specs,
        scratch_shapes=scratch,
        compiler_params=_cparams(("parallel", "parallel", "arbitrary")),
        name=name,
    )(*args)


def _outproj_kernel(*refs, n_lhs, tm, ctx_len):
    lhs = refs[:n_lhs]
    ws = refs[n_lhs:2 * n_lhs]
    x_ref, g_ref, o_ref = refs[2 * n_lhs:]
    i = pl.program_id(1)
    acc = jnp.dot(lhs[0][0], ws[0][...], preferred_element_type=F32)
    for p in range(1, n_lhs):
        acc = acc + jnp.dot(lhs[p][0], ws[p][...], preferred_element_type=F32)
    row = i * tm + lax.broadcasted_iota(I32, (tm, 1), 0)
    g = jnp.where(row < ctx_len, g_ref[0, 0:1, :], g_ref[0, 1:2, :])
    o_ref[0] = x_ref[0] + g * acc


def _outproj(lhs_list, w_list, xa, g_sel, ctx_len):
    b, tt, d = xa.shape
    tm = _pick(tt, (768, 384, 256, 128))
    tn = _pick(d, (512, 256, 128))
    n_lhs = len(lhs_list)
    in_specs = []
    for l in lhs_list:
        in_specs.append(pl.BlockSpec((1, tm, l.shape[2]), lambda bb, i, j: (bb, i, 0)))
    for w in w_list:
        in_specs.append(pl.BlockSpec((w.shape[0], tn), lambda bb, i, j: (0, j)))
    in_specs += [
        pl.BlockSpec((1, tm, tn), lambda bb, i, j: (bb, i, j)),
        pl.BlockSpec((1, 2, tn), lambda bb, i, j: (bb, 0, j)),
    ]
    return pl.pallas_call(
        functools.partial(_outproj_kernel, n_lhs=n_lhs, tm=tm, ctx_len=ctx_len),
        out_shape=jax.ShapeDtypeStruct((b, tt, d), F32),
        grid=(b, tt // tm, d // tn),
        in_specs=in_specs,
        out_specs=pl.BlockSpec((1, tm, tn), lambda bb, i, j: (bb, i, j)),
        compiler_params=_cparams(("parallel", "parallel", "parallel")),
        name="outproj",
    )(*lhs_list, *w_list, xa, g_sel)


def _ret_kernel(*refs, direction, n_ctx_chunks):
    if direction == 0:
        q_ref, k_ref, v_ref, lg_ref, o_ref, st_scr, dm_scr, qd_scr, kd_scr = refs
    else:
        q_ref, k_ref, v_ref, lg_ref, of_ref, g_ref, o_ref, st_scr, dm_scr, qd_scr, kd_scr = refs
    c = pl.program_id(1)
    ii = lax.broadcasted_iota(I32, (CHUNK, CHUNK), 0).astype(F32)
    jj = lax.broadcasted_iota(I32, (CHUNK, CHUNK), 1).astype(F32)

    @pl.when(c == 0)
    def _():
        st_scr[...] = jnp.zeros_like(st_scr)
        for h in range(RET_HEADS):
            lg = -jnp.exp(lg_ref[h])
            if direction == 0:
                diff = ii - jj
                qe = ii + 1.0
                ke = (CHUNK - 1.0) - ii
            else:
                diff = jj - ii
                qe = CHUNK - ii
                ke = ii
            keep = diff >= 0
            dm_scr[h] = jnp.where(keep, jnp.exp(jnp.where(keep, diff, 0.0) * lg), 0.0)
            qd_scr[h] = jnp.exp(qe * lg)
            kd_scr[h] = jnp.exp(ke * lg)

    for h in range(RET_HEADS):
        sl = slice(h * HEAD_DIM, (h + 1) * HEAD_DIM)
        qh = q_ref[0, :, sl]
        kh = k_ref[0, :, sl]
        vh = v_ref[0, :, sl]
        st = st_scr[h]
        s = lax.dot_general(qh, kh, _NT, preferred_element_type=F32) * dm_scr[h]
        inner = jnp.dot(s.astype(BF16), vh, preferred_element_type=F32)
        cross = jnp.dot(qh, st.astype(BF16), preferred_element_type=F32) * qd_scr[h]
        out = inner + cross
        kdec = (kh.astype(F32) * kd_scr[h]).T.astype(BF16)
        cd = jnp.exp(CHUNK * (-jnp.exp(lg_ref[h])))
        st_scr[h] = cd * st + jnp.dot(kdec, vh, preferred_element_type=F32)
        if direction == 0:
            o_ref[0, :, sl] = out
        else:
            o = out + of_ref[0, :, sl]
            mu = jnp.mean(o, axis=-1, keepdims=True)
            var = jnp.mean(jnp.square(o - mu), axis=-1, keepdims=True)
            on = (o - mu) * lax.rsqrt(var + EPS)
            o_ref[0, :, sl] = (on * _silu(g_ref[0, :, sl].astype(F32))).astype(o_ref.dtype)


def _retention(p, lgb, ctx_len):
    b, tt, _ = p.shape
    nc = tt // CHUNK
    ncc = ctx_len // CHUNK
    w = RET_HEADS * HEAD_DIM

    def fwd_chunk(bb, c):
        return c

    def bwd_chunk(bb, c):
        return jnp.where(c < ncc, ncc - 1 - c, nc - 1 + ncc - c)

    def specs(chunk_fn):
        return [
            pl.BlockSpec((1, CHUNK, w), lambda bb, c: (bb, chunk_fn(bb, c), 0)),
            pl.BlockSpec((1, CHUNK, w), lambda bb, c: (bb, chunk_fn(bb, c), 1)),
            pl.BlockSpec((1, CHUNK, w), lambda bb, c: (bb, chunk_fn(bb, c), 2)),
        ]

    scratch = [pltpu.VMEM((RET_HEADS, HEAD_DIM, HEAD_DIM), F32) for _ in range(4)]
    lg_spec = lambda d: pl.BlockSpec((None, RET_HEADS, 1, LANES), lambda bb, c: (d, 0, 0, 0))
    o_f = pl.pallas_call(
        functools.partial(_ret_kernel, direction=0, n_ctx_chunks=ncc),
        out_shape=jax.ShapeDtypeStruct((b, tt, w), F32),
        grid=(b, nc),
        in_specs=specs(fwd_chunk) + [lg_spec(0)],
        out_specs=pl.BlockSpec((1, CHUNK, w), lambda bb, c: (bb, c, 0)),
        scratch_shapes=scratch,
        compiler_params=_cparams(("parallel", "arbitrary")),
        name="retention_fwd",
    )(p, p, p, lgb)
    y = pl.pallas_call(
        functools.partial(_ret_kernel, direction=1, n_ctx_chunks=ncc),
        out_shape=jax.ShapeDtypeStruct((b, tt, w), BF16),
        grid=(b, nc),
        in_specs=specs(bwd_chunk) + [
            lg_spec(1),
            pl.BlockSpec((1, CHUNK, w), lambda bb, c: (bb, bwd_chunk(bb, c), 0)),
            pl.BlockSpec((1, CHUNK, w), lambda bb, c: (bb, bwd_chunk(bb, c), 3)),
        ],
        out_specs=pl.BlockSpec((1, CHUNK, w), lambda bb, c: (bb, bwd_chunk(bb, c), 0)),
        scratch_shapes=scratch,
        compiler_params=_cparams(("parallel", "arbitrary")),
        name="retention_bwd",
    )(p, p, p, lgb, o_f, p)
    return y


def _attn_kernel(q_ref, k_ref, v_ref, o_ref, *, tq, ctx_len):
    i = pl.program_id(2)
    scale_log2e = (HEAD_DIM ** -0.5) * math.log2(math.e)

    def run(masked):
        s = lax.dot_general(q_ref[0], k_ref[0], _NT, preferred_element_type=F32) * scale_log2e
        if masked:
            qrow = i * tq + lax.broadcasted_iota(I32, (tq, 1), 0)
            key = lax.broadcasted_iota(I32, (1, s.shape[1]), 1)
            s = jnp.where((qrow < ctx_len) & (key >= ctx_len), NEG_BIG, s)
        pexp = jnp.exp2(s - jnp.max(s, axis=-1, keepdims=True))
        denom = jnp.sum(pexp, axis=-1, keepdims=True)
        out = jnp.dot(pexp.astype(BF16), v_ref[0], preferred_element_type=F32)
        o_ref[0] = (out / denom).astype(o_ref.dtype)

    has_ctx = i * tq < ctx_len

    @pl.when(has_ctx)
    def _():
        run(True)

    @pl.when(jnp.logical_not(has_ctx))
    def _():
        run(False)


def _attention(p, ctx_len):
    b, tt, _ = p.shape
    tq = _pick(tt, (256, 128))
    groups = ATT_HEADS // ATT_KV_HEADS
    q_base = 4 * RET_HEADS
    k_base = q_base + ATT_HEADS
    v_base = k_base + ATT_KV_HEADS
    return pl.pallas_call(
        functools.partial(_attn_kernel, tq=tq, ctx_len=ctx_len),
        out_shape=jax.ShapeDtypeStruct((b, tt, ATT_HEADS * HEAD_DIM), BF16),
        grid=(b, ATT_KV_HEADS, tt // tq, groups),
        in_specs=[
            pl.BlockSpec((1, tq, HEAD_DIM), lambda bb, kv, i, g: (bb, i, q_base + kv * groups + g)),
            pl.BlockSpec((1, tt, HEAD_DIM), lambda bb, kv, i, g: (bb, 0, k_base + kv)),
            pl.BlockSpec((1, tt, HEAD_DIM), lambda bb, kv, i, g: (bb, 0, v_base + kv)),
        ],
        out_specs=pl.BlockSpec((1, tq, HEAD_DIM), lambda bb, kv, i, g: (bb, i, kv * groups + g)),
        compiler_params=_cparams(("parallel", "parallel", "parallel", "arbitrary")),
        name="gqa_attention",
    )(p, p, p)


def _conv_kernel(main_ref, prev_ref, next_ref, w_ref, b_ref, o_ref, ext_scr, *, tr, halo, ctx_len, tt):
    i = pl.program_id(1)
    start = i * tr
    seg_lo = jnp.where(start < ctx_len, 0, ctx_len)
    seg_hi = jnp.where(start < ctx_len, ctx_len, tt)
    ext = jnp.concatenate([prev_ref[0], main_ref[0], next_ref[0]], axis=0).astype(F32)
    row = start - halo + lax.broadcasted_iota(I32, (tr + 2 * halo, 1), 0)
    ext_scr[...] = jnp.where((row >= seg_lo) & (row < seg_hi), ext, 0.0)
    pad = D_CONV // 2
    acc = b_ref[...] + w_ref[0:1, :] * ext_scr[pl.ds(halo - pad, tr), :]
    for k in range(1, D_CONV):
        acc = acc + w_ref[k:k + 1, :] * ext_scr[pl.ds(halo - pad + k, tr), :]
    o_ref[0] = _silu(acc).astype(o_ref.dtype)


def _conv(p, col0, width, conv_w, conv_b, ctx_len):
    b, tt, _ = p.shape
    tr = CHUNK
    halo = 16
    tc = _pick(math.gcd(width, col0), (2048, 1024, 512, 256, 128))
    assert ctx_len % tr == 0 and col0 % tc == 0
    cb0 = col0 // tc
    nblk = tt // halo
    r = tr // halo
    return pl.pallas_call(
        functools.partial(_conv_kernel, tr=tr, halo=halo, ctx_len=ctx_len, tt=tt),
        out_shape=jax.ShapeDtypeStruct((b, tt, width), BF16),
        grid=(b, tt // tr, width // tc),
        in_specs=[
            pl.BlockSpec((1, tr, tc), lambda bb, i, j: (bb, i, cb0 + j)),
            pl.BlockSpec((1, halo, tc), lambda bb, i, j: (bb, jnp.maximum(i * r - 1, 0), cb0 + j)),
            pl.BlockSpec((1, halo, tc), lambda bb, i, j: (bb, jnp.minimum((i + 1) * r, nblk - 1), cb0 + j)),
            pl.BlockSpec((D_CONV, tc), lambda bb, i, j: (0, j)),
            pl.BlockSpec((1, tc), lambda bb, i, j: (0, j)),
        ],
        out_specs=pl.BlockSpec((1, tr, tc), lambda bb, i, j: (bb, i, j)),
        scratch_shapes=[pltpu.VMEM((tr + 2 * halo, tc), F32)],
        compiler_params=_cparams(("parallel", "parallel", "parallel")),
        name="ssd_conv",
    )(p, p, p, conv_w, conv_b)


def _ssd_kernel(*refs, direction):
    if direction == 0:
        x_ref, b_ref, c_ref, dt_ref, bias_ref, alog_ref, o_ref, st_scr = refs
    else:
        (x_ref, b_ref, c_ref, dt_ref, bias_ref, alog_ref, yf_ref, z_ref, dsk_ref, nw_ref,
         o_ref, st_scr) = refs
    cstep = pl.program_id(1)

    @pl.when(cstep == 0)
    def _():
        st_scr[...] = jnp.zeros_like(st_scr)

    n_heads_dir = LANES // 2
    dt = _softplus(dt_ref[0] + bias_ref[...])
    dt_t = dt.T
    a_col = -jnp.exp(alog_ref[...])
    dta_t = dt_t * a_col
    kk = lax.broadcasted_iota(I32, (CHUNK, CHUNK), 0)
    jj = lax.broadcasted_iota(I32, (CHUNK, CHUNK), 1)
    if direction == 0:
        tri = (kk <= jj).astype(F32)
        keep = kk >= jj
        end_row = CHUNK - 1
    else:
        tri = (kk >= jj).astype(F32)
        keep = kk <= jj
        end_row = 0
    cs_t = jnp.dot(dta_t, tri, preferred_element_type=F32, precision=lax.Precision.HIGHEST)
    cs = cs_t.T
    lane = lax.broadcasted_iota(I32, (1, LANES), 1)
    lo = lane < SSM_HEAD_DIM
    hpg = n_heads_dir // N_GROUPS

    for g in range(N_GROUPS):
        gs = slice(g * D_STATE, (g + 1) * D_STATE)
        cmat = c_ref[0, :, gs]
        bmat = b_ref[0, :, gs]
        b_t = bmat.astype(F32).T.astype(BF16)
        cb = jnp.dot(cmat, b_t, preferred_element_type=F32)
        ysq = jnp.zeros((CHUNK, 1), F32)
        ypairs = []
        for pr in range(hpg // 2):
            h0 = g * hpg + 2 * pr
            hd0 = direction * n_heads_dir + h0
            xs = slice(h0 * SSM_HEAD_DIM, (h0 + 2) * SSM_HEAD_DIM)
            xpair = x_ref[0, :, xs]
            cs_b = []
            dt_b = []
            ydiag = []
            for u in range(2):
                hd = hd0 + u
                cs_col = jnp.broadcast_to(cs[:, hd:hd + 1], (CHUNK, LANES))
                dt_col = jnp.broadcast_to(dt[:, hd:hd + 1], (CHUNK, LANES))
                seg = cs_col - cs_t[hd:hd + 1, :]
                lm = jnp.where(keep, jnp.exp(jnp.where(keep, seg, 0.0)), 0.0)
                wmat = cb * lm * dt_t[hd:hd + 1, :]
                ydiag.append(jnp.dot(wmat.astype(BF16), xpair, preferred_element_type=F32))
                cs_b.append(cs_col)
                dt_b.append(dt_col)
            cs_pair = jnp.where(lo, cs_b[0], cs_b[1])
            dt_pair = jnp.where(lo, dt_b[0], dt_b[1])
            e_pair = jnp.exp(cs_pair)
            tot = cs_pair[end_row:end_row + 1, :]
            st = st_scr[g, pr]
            y_off = jnp.dot(cmat, st.astype(BF16), preferred_element_type=F32) * e_pair
            y = jnp.where(lo, ydiag[0], ydiag[1]) + y_off
            to_end = jnp.exp(tot - cs_pair) * dt_pair
            xw = (xpair.astype(F32) * to_end).astype(BF16)
            st_scr[g, pr] = jnp.exp(tot) * st + jnp.dot(b_t, xw, preferred_element_type=F32)
            if direction == 0:
                o_ref[0, :, xs] = y
            else:
                y = y + yf_ref[0, :, xs] + dsk_ref[:, xs] * xpair.astype(F32)
                y = y * _silu(z_ref[0, :, xs].astype(F32))
                ysq = ysq + jnp.sum(y * y, axis=-1, keepdims=True)
                ypairs.append((xs, y))
        if direction == 1:
            inv = lax.rsqrt(ysq / (hpg * SSM_HEAD_DIM) + EPS)
            for xs, y in ypairs:
                o_ref[0, :, xs] = (y * inv * nw_ref[:, xs]).astype(o_ref.dtype)


def _ssd(xbc, dt_raw, p, dt_bias_row, alog_b, dskip_row, norm_w_row, ctx_len):
    b, tt, _ = xbc.shape
    nc = tt // CHUNK
    ncc = ctx_len // CHUNK
    d_inner = dskip_row.shape[1]
    gw = N_GROUPS * D_STATE
    nb = d_inner // gw

    def fwd_chunk(c):
        return c

    def bwd_chunk(c):
        return jnp.where(c < ncc, ncc - 1 - c, nc - 1 + ncc - c)

    def specs(cf):
        return [
            pl.BlockSpec((1, CHUNK, d_inner), lambda bb, c: (bb, cf(c), 0)),
            pl.BlockSpec((1, CHUNK, gw), lambda bb, c: (bb, cf(c), nb)),
            pl.BlockSpec((1, CHUNK, gw), lambda bb, c: (bb, cf(c), nb + 1)),
            pl.BlockSpec((1, CHUNK, LANES), lambda bb, c: (bb, cf(c), 0)),
            pl.BlockSpec((1, LANES), lambda bb, c: (0, 0)),
            pl.BlockSpec((LANES, LANES), lambda bb, c: (0, 0)),
        ]

    hpg = (d_inner // SSM_HEAD_DIM) // N_GROUPS
    scratch = [pltpu.VMEM((N_GROUPS, hpg // 2, D_STATE, 2 * SSM_HEAD_DIM), F32)]
    y_f = pl.pallas_call(
        functools.partial(_ssd_kernel, direction=0),
        out_shape=jax.ShapeDtypeStruct((b, tt, d_inner), F32),
        grid=(b, nc),
        in_specs=specs(fwd_chunk),
        out_specs=pl.BlockSpec((1, CHUNK, d_inner), lambda bb, c: (bb, c, 0)),
        scratch_shapes=scratch,
        compiler_params=_cparams(("parallel", "arbitrary")),
        name="ssd_fwd",
    )(xbc, xbc, xbc, dt_raw, dt_bias_row, alog_b)
    yn = pl.pallas_call(
        functools.partial(_ssd_kernel, direction=1),
        out_shape=jax.ShapeDtypeStruct((b, tt, d_inner), BF16),
        grid=(b, nc),
        in_specs=specs(bwd_chunk) + [
            pl.BlockSpec((1, CHUNK, d_inner), lambda bb, c: (bb, bwd_chunk(c), 0)),
            pl.BlockSpec((1, CHUNK, d_inner), lambda bb, c: (bb, bwd_chunk(c), 0)),
            pl.BlockSpec((1, d_inner), lambda bb, c: (0, 0)),
            pl.BlockSpec((1, d_inner), lambda bb, c: (0, 0)),
        ],
        out_specs=pl.BlockSpec((1, CHUNK, d_inner), lambda bb, c: (bb, bwd_chunk(c), 0)),
        scratch_shapes=scratch,
        compiler_params=_cparams(("parallel", "arbitrary")),
        name="ssd_bwd",
    )(xbc, xbc, xbc, dt_raw, dt_bias_row, alog_b, y_f, p, dskip_row, norm_w_row)
    return yn


def _topk_rows(sc, n_rows, k):
    rid = lax.broadcasted_iota(I32, sc.shape, 0)
    vals = []
    idxs = []
    for _ in range(k):
        m = jnp.max(sc, axis=0, keepdims=True)
        idx = jnp.min(jnp.where(sc == m, rid, n_rows), axis=0, keepdims=True)
        vals.append(m)
        idxs.append(idx)
        sc = jnp.where(rid == idx, -jnp.inf, sc)
    return jnp.concatenate(vals, axis=0), jnp.concatenate(idxs, axis=0), rid


def _route_kernel(q_ref, keys_ref, ex_ref, gate_ref, ext_scr, gt_scr):
    def head_body(h, carry):
        side = []
        for s in range(2):
            col = pl.multiple_of((h * 2 + s) * HEAD_DIM, HEAD_DIM)
            qs = q_ref[:, pl.ds(col, HEAD_DIM)].astype(BF16)
            ky = keys_ref[h, s]
            sc = lax.dot_general(ky, qs, _NT, preferred_element_type=F32)
            v, ix, _ = _topk_rows(sc, N_KEYS, PEER_TOPK)
            side.append((v, ix))
        (s1, i1), (s2, i2) = side
        r8 = lax.broadcasted_iota(I32, (8, 1), 0)
        blocks = []
        for half in range(2):
            rows = slice(8 * half, 8 * half + 8)
            blocks.append((s1[0:1, :] + s2[rows, :], i1[0:1, :] * N_KEYS + i2[rows, :], 8 * half + r8))
        for a in range(1, 8):
            nb = PEER_TOPK // (a + 1)
            val = s1[a:a + 1, :] + s2[0:8, :]
            if nb < 8:
                val = jnp.where(r8 < nb, val, -jnp.inf)
            blocks.append((val, i1[a:a + 1, :] * N_KEYS + i2[0:8, :], PEER_TOPK * a + r8))
        blocks.append((s1[8:16, :] + s2[0:1, :], i1[8:16, :] * N_KEYS + i2[0:1, :],
                       PEER_TOPK * (8 + r8)))
        cand = jnp.concatenate([blk[0] for blk in blocks], axis=0)
        cidx = jnp.concatenate([blk[1] for blk in blocks], axis=0)
        flat = jnp.concatenate([blk[2] for blk in blocks], axis=0)
        best = []
        exps = []
        for _ in range(PEER_TOPK):
            m = jnp.max(cand, axis=0, keepdims=True)
            pos = jnp.min(jnp.where(cand == m, flat, PEER_TOPK * PEER_TOPK), axis=0, keepdims=True)
            hit = flat == pos
            exps.append(jnp.max(jnp.where(hit, cidx, -1), axis=0, keepdims=True))
            best.append(m)
            cand = jnp.where(hit, -jnp.inf, cand)
        best = jnp.concatenate(best, axis=0)
        e = jnp.exp(best - best[0:1, :])
        gate = e / jnp.sum(e, axis=0, keepdims=True)
        row0 = pl.multiple_of(h * PEER_TOPK, PEER_TOPK)
        ext_scr[pl.ds(row0, PEER_TOPK), :] = jnp.concatenate(exps, axis=0)
        gt_scr[pl.ds(row0, PEER_TOPK), :] = gate
        return carry

    lax.fori_loop(0, PEER_HEADS, head_body, 0, unroll=4)
    ex_ref[...] = ext_scr[...].T
    gate_ref[...] = gt_scr[...].T


def _route(q, keys_bf16):
    n, dq = q.shape
    tl = LANES
    npair = PEER_HEADS * PEER_TOPK
    return pl.pallas_call(
        _route_kernel,
        out_shape=(jax.ShapeDtypeStruct((n, npair), I32), jax.ShapeDtypeStruct((n, npair), F32)),
        grid=(n // tl,),
        in_specs=[
            pl.BlockSpec((tl, dq), lambda i: (i, 0)),
            pl.BlockSpec(keys_bf16.shape, lambda i: (0, 0, 0, 0)),
        ],
        out_specs=(pl.BlockSpec((tl, npair), lambda i: (i, 0)), pl.BlockSpec((tl, npair), lambda i: (i, 0))),
        scratch_shapes=[pltpu.VMEM((npair, tl), I32), pltpu.VMEM((npair, tl), F32)],
        compiler_params=_cparams(("parallel",)),
        name="peer_route",
    )(q, keys_bf16)


ROWS_PER_EXPERT = 16
N_SLOTS = 10
LOOKAHEAD = 8
TOKENS_PER_STEP = 2
GROUP = 8


def _peer_kernel(ex_ref, h_ref, gate_ref, x_ref, gmod_ref, tab_ref, o_ref, buf, sems, w_scr, *, tb, npair):
    rpe = ROWS_PER_EXPERT
    rows = npair * rpe
    n_groups = npair // GROUP
    sub = lax.broadcasted_iota(I32, (8, LANES), 0)
    lane = lax.broadcasted_iota(I32, (8, LANES), 1)
    masks = [(sub & sh) == 0 for sh in (1, 2, 4)]
    hi_mask = jnp.uint32(0xFFFF0000)

    def wait_slot(slot):
        pltpu.make_async_copy(tab_ref.at[pl.ds(0, rows), :], buf.at[slot], sems.at[slot]).wait()

    def issue(t, slot, j0, j1):
        for j in range(j0, j1):
            e = ex_ref[t, j]
            src = tab_ref.at[pl.ds(pl.multiple_of(e * rpe, rpe), rpe), :]
            pltpu.make_async_copy(src, buf.at[slot, pl.ds(j * rpe, rpe), :], sems.at[slot]).start(priority=j % 2)

    def combine(a, b, level):
        m = masks[level]
        return jnp.where(m, a, b) + pltpu.roll(jnp.where(m, b, a), 1 << level, 0)

    def row_to_tile(row, s0):
        tile = jnp.broadcast_to(row[:, s0 * LANES:(s0 + 1) * LANES], (8, LANES))
        for s in range(1, 8):
            piece = jnp.broadcast_to(row[:, (s0 + s) * LANES:(s0 + s + 1) * LANES], (8, LANES))
            tile = jnp.where(sub == s, piece, tile)
        return tile

    def load_token(t):
        hrow = h_ref[pl.ds(t, 1), :]
        h0 = row_to_tile(hrow, 0)
        h1 = row_to_tile(hrow, 8)
        grow = jnp.broadcast_to(gate_ref[pl.ds(t, 1), :], (8, LANES))
        return t, h0, h1, grow

    def load_group(t, g):
        slot = t % N_SLOTS
        return [(buf[slot, pl.ds((g * GROUP + k) * rpe, 8), :], buf[slot, pl.ds((g * GROUP + k) * rpe + 8, 8), :])
                for k in range(GROUP)]

    def group_step(words, tok, g, acc, u):
        _, h0, h1, grow = tok
        acc0, acc1 = acc
        parts = [pltpu.bitcast(x0 << 16, F32) * h0 + pltpu.bitcast(x1 << 16, F32) * h1 for x0, x1 in words]
        for level in range(3):
            parts = [combine(parts[2 * i], parts[2 * i + 1], level) for i in range(len(parts) // 2)]
        act = jnp.sum(parts[0], axis=-1, keepdims=True)
        gcol = jnp.sum(jnp.where(lane == g * GROUP + sub, grow, 0.0), axis=-1, keepdims=True)
        wcol = 0.5 * act * (1.0 + lax.erf(act * (2.0 ** -0.5))) * gcol
        w_scr[u, pl.ds(g * GROUP, GROUP), :] = jnp.broadcast_to(wcol, (GROUP, LANES))
        for k in range(GROUP):
            wb = jnp.broadcast_to(w_scr[u, pl.ds(g * GROUP + k, 1), :], (8, LANES))
            acc0 = acc0 + wb * pltpu.bitcast(words[k][0] & hi_mask, F32)
            acc1 = acc1 + wb * pltpu.bitcast(words[k][1] & hi_mask, F32)
        return acc0, acc1

    def store_token(tok, acc):
        t = tok[0]
        out_row = jnp.concatenate([a[s:s + 1, :] for a in acc for s in range(8)], axis=1)
        o_ref[pl.ds(t, 1), :] = x_ref[pl.ds(t, 1), :] + gmod_ref[0] * out_row

    def tokens(t_first, refill):
        ts = [t_first + u for u in range(TOKENS_PER_STEP)]
        for t in ts:
            wait_slot(t % N_SLOTS)
        toks = [load_token(t) for t in ts]
        accs = [(jnp.zeros((8, LANES), F32), jnp.zeros((8, LANES), F32)) for _ in ts]
        for g in range(n_groups):
            if g == 0:
                pend = [[load_group(t, 0) for t in ts], [load_group(t, 1) for t in ts]]
            words = pend.pop(0)
            if g + 2 < n_groups:
                pend.append([load_group(t, g + 2) for t in ts])
            if refill:
                for t in ts:
                    issue(t + LOOKAHEAD, (t + LOOKAHEAD) % N_SLOTS, g * GROUP, (g + 1) * GROUP)
            accs = [group_step(w, tok, g, acc, u) for u, (w, tok, acc) in enumerate(zip(words, toks, accs))]
        for tok, acc in zip(toks, accs):
            store_token(tok, acc)

    for t0 in range(LOOKAHEAD):
        issue(t0, t0, 0, npair)

    def main_body(it, carry):
        tokens(it * TOKENS_PER_STEP, True)
        return carry

    def tail_body(it, carry):
        tokens(it * TOKENS_PER_STEP, False)
        return carry

    n_main = (tb - LOOKAHEAD) // TOKENS_PER_STEP
    lax.fori_loop(0, n_main, main_body, 0)
    lax.fori_loop(n_main, tb // TOKENS_PER_STEP, tail_body, 0)


def _peer(ex, h2, gate, x2, g_sel, table, tt, ctx_len):
    n, npair = ex.shape
    rpe = ROWS_PER_EXPERT
    d = rpe * LANES
    tb = _pick(math.gcd(tt, ctx_len), (128, 64, 32, 16, 8))
    assert tb >= N_SLOTS and h2.shape == (n, d) and npair % GROUP == 0
    assert (tb - LOOKAHEAD) % TOKENS_PER_STEP == 0 and tb % TOKENS_PER_STEP == 0
    assert N_SLOTS - LOOKAHEAD >= TOKENS_PER_STEP
    g_rows = g_sel.reshape(-1, 1, d)

    def g_index(i):
        r = i * tb
        return ((r // tt) * 2 + jnp.where(r % tt < ctx_len, 0, 1), 0, 0)

    return pl.pallas_call(
        functools.partial(_peer_kernel, tb=tb, npair=npair),
        out_shape=jax.ShapeDtypeStruct((n, d), F32),
        grid=(n // tb,),
        in_specs=[
            pl.BlockSpec((tb, npair), lambda i: (i, 0), memory_space=pltpu.SMEM),
            pl.BlockSpec((tb, d), lambda i: (i, 0)),
            pl.BlockSpec((tb, npair), lambda i: (i, 0)),
            pl.BlockSpec((tb, d), lambda i: (i, 0)),
            pl.BlockSpec((1, 1, d), g_index),
            pl.BlockSpec(memory_space=pl.ANY),
        ],
        out_specs=pl.BlockSpec((tb, d), lambda i: (i, 0)),
        scratch_shapes=[pltpu.VMEM((N_SLOTS, npair * rpe, LANES), jnp.uint32),
                        pltpu.SemaphoreType.DMA((N_SLOTS,)),
                        pltpu.VMEM((TOKENS_PER_STEP, npair, LANES), F32)],
        compiler_params=_cparams(("arbitrary",)),
        name="peer_experts",
    )(ex, h2, gate, x2, g_rows, table)


def _final_kernel(x_ref, w_ref, o_ref):
    x = x_ref[0]
    o_ref[0] = x * lax.rsqrt(jnp.mean(x * x, axis=-1, keepdims=True) + EPS) * w_ref[...]


def _final_norm(xa, w_row, ctx_len):
    b, tt, d = xa.shape
    t = tt - ctx_len
    tr = _pick(math.gcd(t, ctx_len), (256, 128))
    off = ctx_len // tr
    return pl.pallas_call(
        _final_kernel,
        out_shape=jax.ShapeDtypeStruct((b, t, d), F32),
        grid=(b, t // tr),
        in_specs=[pl.BlockSpec((1, tr, d), lambda bb, i: (bb, i + off, 0)),
                  pl.BlockSpec((1, d), lambda bb, i: (0, 0))],
        out_specs=pl.BlockSpec((1, tr, d), lambda bb, i: (bb, i, 0)),
        compiler_params=_cparams(("parallel", "parallel")),
        name="final_norm",
    )(xa, w_row)


def _rope_tables(t, ctx_len):
    tok = jnp.arange(t, dtype=I32)
    row = (tok // GRID_W).astype(F32)
    col = (tok % GRID_W).astype(F32)
    axis_dim = HEAD_DIM // 2
    inv = ROPE_THETA ** (-jnp.arange(0, axis_dim, 2, dtype=F32) / axis_dim)
    ang_r = row[:, None] * inv[None, :]
    ang_c = col[:, None] * inv[None, :]
    cos = jnp.concatenate([jnp.cos(ang_r), jnp.cos(ang_r), jnp.cos(ang_c), jnp.cos(ang_c)], axis=-1)
    sin = jnp.concatenate([-jnp.sin(ang_r), jnp.sin(ang_r), -jnp.sin(ang_c), jnp.sin(ang_c)], axis=-1)
    cos = jnp.concatenate([jnp.ones((ctx_len, HEAD_DIM), F32), cos], axis=0)
    sin = jnp.concatenate([jnp.zeros((ctx_len, HEAD_DIM), F32), sin], axis=0)
    return cos, sin


def _pack_kernel(u_ref, v_ref, o_ref, *, te):
    rpe = ROWS_PER_EXPERT
    for s in range(rpe):
        ls = slice(s * LANES, (s + 1) * LANES)
        ub = pltpu.bitcast(u_ref[:, ls].astype(BF16).astype(F32), jnp.uint32)
        vb = pltpu.bitcast(v_ref[:, ls].astype(BF16).astype(F32), jnp.uint32)
        o_ref[pl.ds(s, te, stride=rpe), :] = (ub >> 16) | vb


def _pack_experts(u, v):
    e, d = u.shape
    rpe = d // LANES
    assert rpe == ROWS_PER_EXPERT
    te = _pick(e, (256, 128, 64, 32, 16, 8))
    return pl.pallas_call(
        functools.partial(_pack_kernel, te=te),
        out_shape=jax.ShapeDtypeStruct((e * rpe, LANES), jnp.uint32),
        grid=(e // te,),
        in_specs=[pl.BlockSpec((te, d), lambda i: (i, 0)), pl.BlockSpec((te, d), lambda i: (i, 0))],
        out_specs=pl.BlockSpec((te * rpe, LANES), lambda i: (i, 0)),
        compiler_params=_cparams(("parallel",)),
        name="pack_experts",
    )(u, v)


def _mod_sel(mods_l, chunk, b, d):
    sl = slice(chunk * d, (chunk + 1) * d)
    ctx_row = jnp.broadcast_to(mods_l[b:b + 1, sl], (b, d))
    return jnp.stack([ctx_row, mods_l[:b, sl]], axis=1)


def kernel(x, c, ctx, c_ctx, ada_w, ada_b, norm1_w, norm2_w, ev_w_in, ev_w_out, ev_ret_decay, ev_q_norm, ev_k_norm, od_w_in, od_conv_w, od_conv_b, od_dt_bias, od_a_log, od_d, od_norm_w, od_w_out, peer_wq, peer_keys, peer_u, peer_v, final_norm_w):
    b, t, d = x.shape
    ctx_len = ctx.shape[1]
    depth = ada_w.shape[0]
    tt = ctx_len + t
    assert b + 1 <= 8 and ctx_len % CHUNK == 0 and t % CHUNK == 0 and t % GRID_W == 0
    assert d == ROWS_PER_EXPERT * LANES

    xa = jnp.concatenate([ctx, x], axis=1)
    crow = jnp.zeros((8, d), F32).at[:b].set(c).at[b].set(c_ctx)
    mods = _mods(crow, ada_w, ada_b)
    cos, sin = _rope_tables(t, ctx_len)

    ret_w = RET_HEADS * HEAD_DIM
    d_inner = od_w_out.shape[1]
    n_ssm_heads = d_inner // SSM_HEAD_DIM
    zx_cols = 2 * d_inner + 2 * N_GROUPS * D_STATE

    for layer in range(depth):
        m = mods[layer]
        sel = [_mod_sel(m, k, b, d) for k in range(N_MOD)]
        j = layer // 2
        nw1 = norm1_w[layer].reshape(1, d)
        if layer % 2 == 0:
            p = _proj(xa, nw1, sel[0], sel[1], ev_w_in[j].astype(BF16), BF16, ctx_len,
                      even_extras=(cos, sin, ev_q_norm[j].reshape(1, HEAD_DIM), ev_k_norm[j].reshape(1, HEAD_DIM)))
            lgb = jnp.broadcast_to(ev_ret_decay[j].astype(F32)[:, :, None, None], (2, RET_HEADS, 1, LANES))
            y_ret = _retention(p, lgb, ctx_len)
            y_att = _attention(p, ctx_len)
            w_out = ev_w_out[j].astype(BF16)
            xa = _outproj([y_ret, y_att], [w_out[:ret_w], w_out[ret_w:]], xa, sel[2], ctx_len)
        else:
            w_in = od_w_in[j]
            p = _proj(xa, nw1, sel[0], sel[1], w_in[:, :zx_cols].astype(BF16), BF16, ctx_len)
            dt_raw = _proj(xa, nw1, sel[0], sel[1], w_in[:, zx_cols:].astype(BF16), F32, ctx_len)
            xbc = _conv(p, d_inner, zx_cols - d_inner, od_conv_w[j], od_conv_b[j].reshape(1, -1), ctx_len)
            alog_b = jnp.broadcast_to(od_a_log[j].astype(F32).reshape(2 * n_ssm_heads, 1), (2 * n_ssm_heads, LANES))
            dskip_row = jnp.repeat(od_d[j].astype(F32), SSM_HEAD_DIM).reshape(1, d_inner)
            yn = _ssd(xbc, dt_raw, p, od_dt_bias[j].reshape(1, 2 * n_ssm_heads), alog_b, dskip_row,
                      od_norm_w[j].reshape(1, d_inner), ctx_len)
            xa = _outproj([yn], [od_w_out[j].astype(BF16)], xa, sel[2], ctx_len)

        q, h2 = _proj(xa, norm2_w[layer].reshape(1, d), sel[3], sel[4], peer_wq[layer].astype(BF16), F32,
                      ctx_len, emit_h=True)
        n = b * tt
        ex, gate = _route(q.reshape(n, -1), peer_keys[layer].astype(BF16))
        table = _pack_experts(peer_u[layer], peer_v[layer])
        xa = _peer(ex, h2.reshape(n, d), gate, xa.reshape(n, d), sel[5], table, tt, ctx_len).reshape(b, tt, d)

    return _final_norm(xa, final_norm_w.reshape(1, d), ctx_len)
```
